```python
import math
import jax, jax.numpy as jnp
from jax import lax
import numpy as np

D_MODEL = 1024
BATCH = 8
SEQ = 2048
DEPTH = 4
DEC_BATCH = 128
DEC_SEQ = 1
PAST_LEN = 2048
PAGE_SIZE = 128

HEAD_DIM = 64
HEADS_PER_GROUP = 4
DIL_GROUPS = ((128, 1), (512, 4), (2048, 16))
N_ATT_GROUPS = len(DIL_GROUPS)
ATT_WIDTH = N_ATT_GROUPS * HEADS_PER_GROUP * HEAD_DIM
ATT_OUT = HEADS_PER_GROUP * HEAD_DIM
QBLOCK = 128
POOL_WINDOWS = (2, 4, 8, 16)
N_POOL_GROUPS = len(POOL_WINDOWS)
POOL_WIDTH = D_MODEL // 2
POOL_GROUP = POOL_WIDTH // N_POOL_GROUPS
POOL_BUF = max(POOL_WINDOWS) - 1
SSM_WIDTH = D_MODEL // 2
SSM_GROUP_CH = 16
SSM_GROUPS = SSM_WIDTH // SSM_GROUP_CH
SSM_STATE = 64
N_BRANCH = 3
IN_COLS = 3 * ATT_WIDTH + POOL_WIDTH + SSM_WIDTH + N_BRANCH * D_MODEL
SPLIT_POINTS = (ATT_WIDTH, 2 * ATT_WIDTH, 3 * ATT_WIDTH,
                3 * ATT_WIDTH + POOL_WIDTH, 3 * ATT_WIDTH + POOL_WIDTH + SSM_WIDTH)
D_FF = ((8 * D_MODEL + 3 * 256 - 1) // (3 * 256)) * 256
ALPHA = (2 * DEPTH) ** 0.25
BETA = (8 * DEPTH) ** -0.25
LN_EPS = 1e-5

kernel_name = 'hybrid_dilated_pool_s5_decoder_step'

F32 = jnp.float32


def layer_norm(x, g, b):
    xf = x.astype(F32)
    mu = jnp.mean(xf, axis=-1, keepdims=True)
    var = jnp.mean(jnp.square(xf - mu), axis=-1, keepdims=True)
    return ((xf - mu) * lax.rsqrt(var + LN_EPS) * g.astype(F32) + b.astype(F32)).astype(x.dtype)


def alibi_slopes():
    n = N_ATT_GROUPS * HEADS_PER_GROUP
    s = 2.0 ** (-8.0 * jnp.arange(1, n + 1, dtype=F32) / n)
    return s.reshape(N_ATT_GROUPS, HEADS_PER_GROUP)


def dilated_attn_prompt(q, k, v, dilation, n_back, slopes):
    B, L, H, E = q.shape
    lsub = L // dilation
    qb = min(QBLOCK, lsub)
    nblk = -(-lsub // qb)
    lpad = nblk * qb
    nk = qb + n_back

    def residues(t):
        return t.reshape(B, lsub, dilation, H, E).transpose(0, 2, 1, 3, 4)

    qs = jnp.pad(residues(q), ((0, 0), (0, 0), (0, lpad - lsub), (0, 0), (0, 0)))
    qs = qs.reshape(B, dilation, nblk, qb, H, E)
    pad_kv = ((0, 0), (0, 0), (n_back, lpad - lsub), (0, 0), (0, 0))
    ks = jnp.pad(residues(k), pad_kv)
    vs = jnp.pad(residues(v), pad_kv)
    kidx = jnp.arange(nblk)[:, None] * qb + jnp.arange(nk)[None, :]
    kb = ks[:, :, kidx]
    vb = vs[:, :, kidx]
    s = jnp.einsum('bdnqhe,bdnkhe->bdnhqk', qs.astype(F32), kb.astype(F32)) * (E ** -0.5)
    jd = jnp.arange(qb)[:, None] - jnp.arange(nk)[None, :] + n_back
    key_row = kidx - n_back
    valid = (jd >= 0) & (jd <= n_back) & (key_row[:, None, :] >= 0)
    bias = -slopes.astype(F32)[:, None, None] * (jd * dilation).astype(F32)
    s = jnp.where(valid[:, None], s + bias, -jnp.inf)
    lse = jax.nn.logsumexp(s, axis=-1)
    p = jnp.exp(s - lse[..., None])
    o = jnp.einsum('bdnhqk,bdnkhe->bdnqhe', p, vb.astype(F32))
    o = o.reshape(B, dilation, lpad, H, E)[:, :, :lsub]
    o = o.transpose(0, 2, 1, 3, 4).reshape(B, L, H, E)
    lse = lse.transpose(0, 1, 2, 4, 3).reshape(B, dilation, lpad, H)[:, :, :lsub]
    lse = lse.transpose(0, 2, 1, 3).reshape(B, L, H)
    return o, lse


def dilated_attn_decode(q, k_all, v_all, dilation, n_back, slopes):
    S = q.shape[1]
    lb = k_all.shape[1] - S
    j = jnp.arange(n_back + 1)
    idx = lb + jnp.arange(S)[:, None] - j[None, :] * dilation
    valid = idx >= 0
    idxc = jnp.maximum(idx, 0)
    kg = k_all[:, idxc].astype(F32)
    vg = v_all[:, idxc].astype(F32)
    s = jnp.einsum('bshe,bsjhe->bhsj', q.astype(F32), kg) * (q.shape[-1] ** -0.5)
    s = s - slopes.astype(F32)[None, :, None, None] * (j * dilation).astype(F32)[None, None, None, :]
    s = jnp.where(valid[None, None], s, -jnp.inf)
    lse = jax.nn.logsumexp(s, axis=-1)
    p = jnp.exp(s - lse[..., None])
    o = jnp.einsum('bhsj,bsjhe->bshe', p, vg)
    return o, lse.transpose(0, 2, 1)


def combine_groups(outs, lses):
    o = jnp.stack(outs, axis=0)
    w = jax.nn.softmax(jnp.stack(lses, axis=0), axis=0)
    o = jnp.sum(w[..., None] * o, axis=0)
    B, L = o.shape[:2]
    return o.reshape(B, L, ATT_OUT)


def pool_mix(u, buf, pos0, w_pool, scale):
    B, L, C = u.shape
    nb = buf.shape[1]
    uf = u.astype(F32)
    ext = jnp.concatenate([buf.astype(F32), uf], axis=1)
    cs = jnp.concatenate([jnp.zeros((B, 1, C), F32), jnp.cumsum(ext, axis=1)], axis=1)
    end = nb + 1 + jnp.arange(L)
    n_real = pos0 + 1 + jnp.arange(L)
    means = []
    for g, w in enumerate(POOL_WINDOWS):
        csg = cs[:, :, g * POOL_GROUP:(g + 1) * POOL_GROUP]
        tot = csg[:, end] - csg[:, end - w]
        cnt = jnp.minimum(n_real, w).astype(F32)
        means.append(tot / cnt[None, :, None])
    diff = (jnp.concatenate(means, axis=-1) - uf).reshape(B, L, N_POOL_GROUPS, POOL_GROUP)
    y = jnp.einsum('blgc,gce->blge', diff, w_pool.astype(F32)).reshape(B, L, POOL_WIDTH)
    return y * scale.astype(F32)


def s5_mix(u, h0_re, h0_im, lam_re, lam_im, b_re, b_im, c_re, c_im, d_skip, log_step, w_glu):
    B, L, _ = u.shape
    lam = lax.complex(lam_re.astype(F32), lam_im.astype(F32))
    step = jnp.exp(log_step.astype(F32))[:, None]
    lam_bar = jnp.exp(lam * step)
    bmat = lax.complex(b_re.astype(F32), b_im.astype(F32))
    b_bar = ((lam_bar - 1.0) / lam)[..., None] * bmat
    uf = u.astype(F32)
    ug = uf.reshape(B, L, SSM_GROUPS, SSM_GROUP_CH)
    bu = jnp.einsum('gph,blgh->blgp', b_bar, ug.astype(jnp.complex64))
    a = jnp.broadcast_to(lam_bar, bu.shape)

    def combine(e1, e2):
        a1, b1 = e1
        a2, b2 = e2
        return a1 * a2, a2 * b1 + b2

    a_cum, b_cum = lax.associative_scan(combine, (a, bu), axis=1)
    h0 = lax.complex(h0_re.astype(F32), h0_im.astype(F32))
    h = a_cum * h0[:, None] + b_cum
    cmat = lax.complex(c_re.astype(F32), c_im.astype(F32))
    y = jnp.real(jnp.einsum('ghp,blgp->blgh', cmat, h)).reshape(B, L, SSM_WIDTH)
    y = jax.nn.gelu(y + d_skip.astype(F32) * uf)
    y = y * jax.nn.sigmoid(y @ w_glu.astype(F32))
    h_last = h[:, -1]
    return y, jnp.real(h_last), jnp.imag(h_last)


def trunk_layer(x, pos0, kv_cache, pool_buf, h0_re, h0_im, p):
    B, L, _ = x.shape
    proj = x @ p['w_in']
    q, k, v, u_pool, u_ssm, g_logits = jnp.split(proj, SPLIT_POINTS, axis=-1)
    hshape = (B, L, N_ATT_GROUPS, HEADS_PER_GROUP, HEAD_DIM)
    q, k, v = q.reshape(hshape), k.reshape(hshape), v.reshape(hshape)
    slopes = alibi_slopes()
    outs, lses, new_state = [], [], []
    for g, (window, dil) in enumerate(DIL_GROUPS):
        n_back = window // dil
        qg, kg, vg = q[:, :, g], k[:, :, g], v[:, :, g]
        if kv_cache is None:
            o, lse = dilated_attn_prompt(qg, kg, vg, dil, n_back, slopes[g])
            keep = min(window, L)
            new_state += [kg[:, L - keep:], vg[:, L - keep:]]
        else:
            k_all = jnp.concatenate([kv_cache[2 * g].astype(kg.dtype), kg], axis=1)
            v_all = jnp.concatenate([kv_cache[2 * g + 1].astype(vg.dtype), vg], axis=1)
            o, lse = dilated_attn_decode(qg, k_all, v_all, dil, n_back, slopes[g])
            new_state += [kg, vg]
        outs.append(o)
        lses.append(lse)
    y_att = combine_groups(outs, lses)

    y_pool = pool_mix(u_pool, pool_buf, pos0, p['pool_w'], p['pool_scale'])
    new_state.append(u_pool[:, L - min(POOL_BUF, L):] if kv_cache is None else u_pool)

    y_ssm, h_re, h_im = s5_mix(u_ssm, h0_re, h0_im, p['ssm_lambda_re'], p['ssm_lambda_im'],
                               p['ssm_b_re'], p['ssm_b_im'], p['ssm_c_re'], p['ssm_c_im'],
                               p['ssm_d'], p['ssm_log_step'], p['ssm_w_glu'])
    new_state += [h_re, h_im]

    gates = jax.nn.sigmoid(g_logits.astype(F32)).reshape(B, L, N_BRANCH, D_MODEL)
    merged = (gates[:, :, 0] * (y_att @ p['w_br_att'].astype(F32))
              + gates[:, :, 1] * (y_pool @ p['w_br_pool'].astype(F32))
              + gates[:, :, 2] * (y_ssm @ p['w_br_ssm'].astype(F32)))
    mix = (merged @ p['w_out'].astype(F32)).astype(x.dtype)
    x = layer_norm(ALPHA * x + mix, p['ln1_g'], p['ln1_b'])
    f = (jax.nn.silu(x @ p['ffn_w_gate']) * (x @ p['ffn_w_up'])) @ p['ffn_w_down']
    x = layer_norm(ALPHA * x + f, p['ln2_g'], p['ln2_b'])
    return x, new_state


def setup_inputs(seed: int = 0) -> dict:
    key = jax.random.key(seed)
    ks = iter(jax.random.split(key, 40))

    def nrm(shape, scale):
        return scale * jax.random.normal(next(ks), shape, F32)

    hd = (HEADS_PER_GROUP, HEAD_DIM)
    lb128, lb512, lb2048 = (min(w, PAST_LEN) for w, _ in DIL_GROUPS)
    lam_im0 = math.pi * jnp.arange(SSM_STATE, dtype=F32)
    inp = {}
    inp['x_prompt'] = nrm((BATCH, SEQ, D_MODEL), 1.0)
    inp['x_sample'] = nrm((DEC_BATCH, DEC_SEQ, D_MODEL), 1.0)
    inp['cache_k_w128'] = nrm((DEPTH, DEC_BATCH, lb128) + hd, 1.0)
    inp['cache_v_w128'] = nrm((DEPTH, DEC_BATCH, lb128) + hd, 1.0)
    inp['cache_k_w512'] = nrm((DEPTH, DEC_BATCH, lb512) + hd, 1.0)
    inp['cache_v_w512'] = nrm((DEPTH, DEC_BATCH, lb512) + hd, 1.0)
    inp['cache_k_w2048'] = nrm((DEPTH, DEC_BATCH, lb2048) + hd, 1.0)
    inp['cache_v_w2048'] = nrm((DEPTH, DEC_BATCH, lb2048) + hd, 1.0)
    inp['state_pool'] = nrm((DEPTH, DEC_BATCH, POOL_BUF, POOL_WIDTH), 1.0)
    inp['state_ssm_re'] = nrm((DEPTH, DEC_BATCH, SSM_GROUPS, SSM_STATE), 0.5)
    inp['state_ssm_im'] = nrm((DEPTH, DEC_BATCH, SSM_GROUPS, SSM_STATE), 0.5)
    inp['w_in'] = nrm((DEPTH, D_MODEL, IN_COLS), D_MODEL ** -0.5)
    inp['pool_w'] = nrm((DEPTH, N_POOL_GROUPS, POOL_GROUP, POOL_GROUP), POOL_GROUP ** -0.5)
    inp['pool_scale'] = 1.0 + nrm((DEPTH, POOL_WIDTH), 0.02)
    inp['ssm_lambda_re'] = -0.5 + nrm((DEPTH, SSM_GROUPS, SSM_STATE), 0.01)
    inp['ssm_lambda_im'] = lam_im0 + nrm((DEPTH, SSM_GROUPS, SSM_STATE), 0.01)
    inp['ssm_b_re'] = nrm((DEPTH, SSM_GROUPS, SSM_STATE, SSM_GROUP_CH), (2 * SSM_GROUP_CH) ** -0.5)
    inp['ssm_b_im'] = nrm((DEPTH, SSM_GROUPS, SSM_STATE, SSM_GROUP_CH), (2 * SSM_GROUP_CH) ** -0.5)
    inp['ssm_c_re'] = nrm((DEPTH, SSM_GROUPS, SSM_GROUP_CH, SSM_STATE), (2 * SSM_STATE) ** -0.5)
    inp['ssm_c_im'] = nrm((DEPTH, SSM_GROUPS, SSM_GROUP_CH, SSM_STATE), (2 * SSM_STATE) ** -0.5)
    inp['ssm_d'] = nrm((DEPTH, SSM_WIDTH), 1.0)
    inp['ssm_log_step'] = jax.random.uniform(next(ks), (DEPTH, SSM_GROUPS), F32,
                                             math.log(0.001), math.log(0.1))
    inp['ssm_w_glu'] = nrm((DEPTH, SSM_WIDTH, SSM_WIDTH), SSM_WIDTH ** -0.5)
    inp['w_br_att'] = nrm((DEPTH, ATT_OUT, D_MODEL), ATT_OUT ** -0.5)
    inp['w_br_pool'] = nrm((DEPTH, POOL_WIDTH, D_MODEL), POOL_WIDTH ** -0.5)
    inp['w_br_ssm'] = nrm((DEPTH, SSM_WIDTH, D_MODEL), SSM_WIDTH ** -0.5)
    inp['w_out'] = nrm((DEPTH, D_MODEL, D_MODEL), BETA * D_MODEL ** -0.5)
    inp['ln1_g'] = 1.0 + nrm((DEPTH, D_MODEL), 0.02)
    inp['ln1_b'] = nrm((DEPTH, D_MODEL), 0.02)
    inp['ffn_w_gate'] = nrm((DEPTH, D_MODEL, D_FF), D_MODEL ** -0.5)
    inp['ffn_w_up'] = nrm((DEPTH, D_MODEL, D_FF), D_MODEL ** -0.5)
    inp['ffn_w_down'] = nrm((DEPTH, D_FF, D_MODEL), BETA * D_FF ** -0.5)
    inp['ln2_g'] = 1.0 + nrm((DEPTH, D_MODEL), 0.02)
    inp['ln2_b'] = nrm((DEPTH, D_MODEL), 0.02)
    return inp


def reference(x_prompt, x_sample, cache_k_w128, cache_v_w128, cache_k_w512, cache_v_w512,
              cache_k_w2048, cache_v_w2048, state_pool, state_ssm_re, state_ssm_im,
              w_in, pool_w, pool_scale, ssm_lambda_re, ssm_lambda_im, ssm_b_re, ssm_b_im,
              ssm_c_re, ssm_c_im, ssm_d, ssm_log_step, ssm_w_glu, w_br_att, w_br_pool,
              w_br_ssm, w_out, ln1_g, ln1_b, ffn_w_gate, ffn_w_up, ffn_w_down, ln2_g, ln2_b):
    yp, ys = x_prompt, x_sample
    st_p = [[] for _ in range(9)]
    st_s = [[] for _ in range(9)]
    zero_pool = jnp.zeros((x_prompt.shape[0], POOL_BUF, POOL_WIDTH), x_prompt.dtype)
    zero_h = jnp.zeros((x_prompt.shape[0], SSM_GROUPS, SSM_STATE), F32)
    for l in range(DEPTH):
        p = dict(w_in=w_in[l], pool_w=pool_w[l], pool_scale=pool_scale[l],
                 ssm_lambda_re=ssm_lambda_re[l], ssm_lambda_im=ssm_lambda_im[l],
                 ssm_b_re=ssm_b_re[l], ssm_b_im=ssm_b_im[l], ssm_c_re=ssm_c_re[l],
                 ssm_c_im=ssm_c_im[l], ssm_d=ssm_d[l], ssm_log_step=ssm_log_step[l],
                 ssm_w_glu=ssm_w_glu[l], w_br_att=w_br_att[l], w_br_pool=w_br_pool[l],
                 w_br_ssm=w_br_ssm[l], w_out=w_out[l], ln1_g=ln1_g[l], ln1_b=ln1_b[l],
                 ffn_w_gate=ffn_w_gate[l], ffn_w_up=ffn_w_up[l], ffn_w_down=ffn_w_down[l],
                 ln2_g=ln2_g[l], ln2_b=ln2_b[l])
        yp, new_p = trunk_layer(yp, 0, None, zero_pool, zero_h, zero_h, p)
        kv = [cache_k_w128[l], cache_v_w128[l], cache_k_w512[l], cache_v_w512[l],
              cache_k_w2048[l], cache_v_w2048[l]]
        ys, new_s = trunk_layer(ys, PAST_LEN, kv, state_pool[l], state_ssm_re[l],
                                state_ssm_im[l], p)
        for i in range(9):
            st_p[i].append(new_p[i])
            st_s[i].append(new_s[i])
    (k128_p, v128_p, k512_p, v512_p, k2048_p, v2048_p,
     pool_p, ssm_re_p, ssm_im_p) = [jnp.stack(s, axis=0) for s in st_p]
    (k128_s, v128_s, k512_s, v512_s, k2048_s, v2048_s,
     pool_s, ssm_re_s, ssm_im_s) = [jnp.stack(s, axis=0) for s in st_s]
    return (yp, ys,
            k128_p, v128_p, k512_p, v512_p, k2048_p, v2048_p, pool_p, ssm_re_p, ssm_im_p,
            k128_s, v128_s, k512_s, v512_s, k2048_s, v2048_s, pool_s, ssm_re_s, ssm_im_s)
```

```python
import functools
import math

import numpy as np
import jax
import jax.numpy as jnp
from jax import lax
from jax.experimental import pallas as pl
from jax.experimental.pallas import tpu as pltpu

F32 = jnp.float32
BF16 = jnp.bfloat16

D_MODEL = 1024
DEPTH = 4
PAST_LEN = 2048
HEAD_DIM = 64
HEADS = 4
DIL_GROUPS = ((128, 1), (512, 4), (2048, 16))
N_GROUPS = len(DIL_GROUPS)
N_BACK = 128
GROUP_W = HEADS * HEAD_DIM
ATT_W = N_GROUPS * GROUP_W
QKV_W = 3 * ATT_W
POOL_WINDOWS = (2, 4, 8, 16)
POOL_W = 512
POOL_G = 128
POOL_BUF = 15
POOL_HALO = 16
SSM_W = 512
SSM_GROUPS = 32
SSM_CH = 16
SSM_STATE = 64
SSM_N = SSM_GROUPS * SSM_STATE
D_FF = 2816
FF_CHUNK = 1408
ALPHA = (2 * DEPTH) ** 0.25
LN_EPS = 1e-5
SM_SCALE = HEAD_DIM ** -0.5

VMEM_LIMIT = 56 * 1024 * 1024
MXU_TILE = 256
LANES = 128
SUBLANES = 8

_ALIBI = (2.0 ** (-8.0 * np.arange(1, N_GROUPS * HEADS + 1, dtype=np.float64) / (N_GROUPS * HEADS))
          ).astype(np.float32).reshape(N_GROUPS, HEADS)


def _params(sem):
    return pltpu.CompilerParams(dimension_semantics=sem, vmem_limit_bytes=VMEM_LIMIT)


def _tm_spec(tm, c, col=0, ncol=1):
    return pl.BlockSpec((tm, c), lambda b, t: (t, b * ncol + col))


def _bm_spec(tm, c, col=0):
    return pl.BlockSpec((None, tm, c), lambda b, t: (b, t, col))


def _const_spec(shape):
    zeros = (0,) * len(shape)
    return pl.BlockSpec(shape, lambda *_: zeros, pipeline_mode=pl.Buffered(1))


def _dot(a, b):
    return jnp.dot(a, b, preferred_element_type=F32)


def _layer_norm(z, g, b):
    mu = jnp.mean(z, axis=-1, keepdims=True)
    zc = z - mu
    var = jnp.mean(zc * zc, axis=-1, keepdims=True)
    return zc * lax.rsqrt(var + LN_EPS) * g + b


def _copy_body(x_ref, o_ref):
    o_ref[...] = x_ref[...]


def _to_time_major(x, tm):
    bx, lx, c = x.shape
    return pl.pallas_call(
        _copy_body,
        grid=(bx, lx // tm),
        in_specs=[_bm_spec(tm, c)],
        out_specs=_tm_spec(tm, c),
        out_shape=jax.ShapeDtypeStruct((lx, bx * c), x.dtype),
        compiler_params=_params(("parallel", "parallel")),
        name="to_time_major",
    )(x)


def _proj_body(x_ref, w_ref, o_ref):
    o_ref[...] = _dot(x_ref[...].astype(BF16), w_ref[...])


def _project(x_tm, w, bx, tm):
    lx = x_tm.shape[0]
    n = w.shape[1]
    return pl.pallas_call(
        _proj_body,
        grid=(bx, lx // tm),
        in_specs=[_tm_spec(tm, D_MODEL), _const_spec(w.shape)],
        out_specs=_bm_spec(tm, n),
        out_shape=jax.ShapeDtypeStruct((bx, lx, n), F32),
        compiler_params=_params(("parallel", "parallel")),
        name="project",
    )(x_tm, w)


def _attn_body(q_ref, kp_ref, kc_ref, vp_ref, vc_ref, o_ref, lse_ref, *, dil, slopes):
    first = pl.program_id(2) == 0
    nq, nk = N_BACK, 2 * N_BACK
    q = q_ref[...]
    kk = jnp.concatenate([kp_ref[...], kc_ref[...]], axis=0).astype(BF16)
    vv = jnp.concatenate([vp_ref[...], vc_ref[...]], axis=0).astype(BF16)
    qi = lax.broadcasted_iota(jnp.int32, (nq, nk), 0)
    ki = lax.broadcasted_iota(jnp.int32, (nq, nk), 1)
    back = qi + N_BACK - ki
    valid = (back >= 0) & (back <= N_BACK) & ((ki >= N_BACK) | jnp.logical_not(first))
    dist = (back * dil).astype(F32)
    lane = lax.broadcasted_iota(jnp.int32, (1, GROUP_W), 1)
    o_acc = jnp.zeros((nq, GROUP_W), F32)
    lse_acc = jnp.zeros((nq, GROUP_W), F32)
    for h in range(HEADS):
        head = (lane >= h * HEAD_DIM) & (lane < (h + 1) * HEAD_DIM)
        qh = jnp.where(head, q, 0.0).astype(BF16)
        s = lax.dot_general(qh, kk, (((1,), (1,)), ((), ())), preferred_element_type=F32)
        s = jnp.where(valid, s * SM_SCALE - float(slopes[h]) * dist, -jnp.inf)
        m = jnp.max(s, axis=-1, keepdims=True)
        p = jnp.exp(s - m)
        l = jnp.sum(p, axis=-1, keepdims=True)
        oh = _dot(p.astype(BF16), vv)
        o_acc = jnp.where(head, oh / l, o_acc)
        lse_acc = jnp.where(head, m + jnp.log(l), lse_acc)
    o_ref[...] = o_acc
    lse_ref[...] = lse_acc


def _attn_prompt(qkv, g):
    bx, lx, _ = qkv.shape
    window, dil = DIL_GROUPS[g]
    assert window // dil == N_BACK and lx % (dil * N_BACK) == 0
    lsub = lx // dil
    nblk = lsub // N_BACK
    ncb = QKV_W // GROUP_W
    view = qkv.reshape(bx, lsub, dil * QKV_W)

    def spec(which, prev):
        def idx(b, r, n):
            return (b, jnp.maximum(n - 1, 0) if prev else n, r * ncb + which * N_GROUPS + g)
        return pl.BlockSpec((None, N_BACK, GROUP_W), idx)

    out_spec = pl.BlockSpec((None, N_BACK, GROUP_W), lambda b, r, n: (b, n, r))
    out_sds = jax.ShapeDtypeStruct((bx, lsub, dil * GROUP_W), F32)
    o, lse = pl.pallas_call(
        functools.partial(_attn_body, dil=dil, slopes=_ALIBI[g]),
        grid=(bx, dil, nblk),
        in_specs=[spec(0, False), spec(1, True), spec(1, False), spec(2, True), spec(2, False)],
        out_specs=[out_spec, out_spec],
        out_shape=[out_sds, out_sds],
        compiler_params=_params(("parallel", "parallel", "parallel")),
        name=f"attn_prompt_g{g}",
    )(view, view, view, view, view)
    return o.reshape(bx, lx, GROUP_W), lse.reshape(bx, lx, GROUP_W)


def _pool_body(x_ref, win_ref, pw_ref, sc_ref, y_ref, st_ref, carry_ref, *, tm):
    t = pl.program_id(1)

    @pl.when(t == 0)
    def _():
        carry_ref[...] = jnp.zeros_like(carry_ref)

    u = _dot(x_ref[...].astype(BF16), win_ref[...])
    e = jnp.concatenate([carry_ref[...], u], axis=0)
    sums = []
    s = e
    for level, shift in enumerate((1, 2, 4, 8)):
        s = s[:, POOL_G:] if level else s
        s = s + pltpu.roll(s, shift, 0)
        sums.append(s[POOL_HALO:, :POOL_G])
    n_real = t * tm + 1 + lax.broadcasted_iota(jnp.int32, (tm, 1), 0)
    ys = []
    for g, w in enumerate(POOL_WINDOWS):
        cnt = jnp.minimum(n_real, w).astype(F32)
        diff = sums[g] / cnt - u[:, g * POOL_G:(g + 1) * POOL_G]
        ys.append(_dot(diff.astype(BF16), pw_ref[g]))
    y = jnp.concatenate(ys, axis=-1) * sc_ref[...]
    y_ref[...] = y.astype(BF16)
    carry_ref[...] = u[tm - POOL_HALO:, :]
    st_ref[...] = u[tm - POOL_HALO:, :]


def _pool_prompt(x_tm, w_in_pool, pool_w, pool_scale, bx, tm):
    lx = x_tm.shape[0]
    return pl.pallas_call(
        functools.partial(_pool_body, tm=tm),
        grid=(bx, lx // tm),
        in_specs=[_tm_spec(tm, D_MODEL), _const_spec(w_in_pool.shape), _const_spec(pool_w.shape),
                  _const_spec(pool_scale.shape)],
        out_specs=[_bm_spec(tm, POOL_W),
                   pl.BlockSpec((None, POOL_HALO, POOL_W), lambda b, t: (b, 0, 0))],
        out_shape=[jax.ShapeDtypeStruct((bx, lx, POOL_W), BF16),
                   jax.ShapeDtypeStruct((bx, POOL_HALO, POOL_W), F32)],
        scratch_shapes=[pltpu.VMEM((POOL_HALO, POOL_W), F32)],
        compiler_params=_params(("parallel", "arbitrary")),
        name="pool_prompt",
    )(x_tm, w_in_pool, pool_w, pool_scale)


def _ssm_prep_body(lre_ref, lim_ref, ls_ref, bre_ref, bim_ref, are_ref, aim_ref, bbr_ref, bbi_ref):
    lr, li = lre_ref[...], lim_ref[...]
    step = jnp.exp(ls_ref[...])
    mag = jnp.exp(lr * step)
    ar = mag * jnp.cos(li * step)
    ai = mag * jnp.sin(li * step)
    den = lr * lr + li * li
    fr = ((ar - 1.0) * lr + ai * li) / den
    fi = (ai * lr - (ar - 1.0) * li) / den
    br, bi = bre_ref[...], bim_ref[...]
    are_ref[...] = ar
    aim_ref[...] = ai
    bbr_ref[...] = fr * br - fi * bi
    bbi_ref[...] = fr * bi + fi * br


def _ssm_prep(lam_re, lam_im, log_step, b_re, b_im):
    row = pl.BlockSpec((None, 1, SSM_N), lambda l: (l, 0, 0))
    mat = pl.BlockSpec((None, SSM_CH, SSM_N), lambda l: (l, 0, 0))
    row_sds = jax.ShapeDtypeStruct(lam_re.shape, F32)
    mat_sds = jax.ShapeDtypeStruct(b_re.shape, F32)
    return pl.pallas_call(
        _ssm_prep_body,
        grid=(lam_re.shape[0],),
        in_specs=[row, row, row, mat, mat],
        out_specs=[row, row, mat, mat],
        out_shape=[row_sds, row_sds, mat_sds, mat_sds],
        name="ssm_prep",
    )(lam_re, lam_im, log_step, b_re, b_im)


def _ssm_input(ub, bre_ref, bim_ref):
    res_re, res_im = [], []
    for j in range(SSM_N // MXU_TILE):
        k0 = (j * MXU_TILE // SSM_STATE * SSM_CH) // LANES * LANES
        uk = ub[:, k0:k0 + LANES]
        cols = slice(j * MXU_TILE, (j + 1) * MXU_TILE)
        res_re.append(_dot(uk, bre_ref[k0:k0 + LANES, cols]))
        res_im.append(_dot(uk, bim_ref[k0:k0 + LANES, cols]))
    return res_re, res_im


def _ssm_output(h_re, h_im, u, cre_ref, cim_ref, d_ref, wg_ref):
    n_slab = SSM_W // LANES
    acc = [None] * n_slab
    for j in range(SSM_N // MXU_TILE):
        n0 = (j * MXU_TILE // SSM_STATE * SSM_CH) // LANES * LANES
        rows = slice(j * MXU_TILE, (j + 1) * MXU_TILE)
        c = _dot(h_re(j), cre_ref[rows, n0:n0 + LANES]) - _dot(h_im(j), cim_ref[rows, n0:n0 + LANES])
        i = n0 // LANES
        acc[i] = c if acc[i] is None else acc[i] + c
    y = jnp.concatenate(acc, axis=-1)
    y = jax.nn.gelu(y + d_ref[...] * u)
    return y * jax.nn.sigmoid(_dot(y.astype(BF16), wg_ref[...]))


def _ssm_body(x_ref, win_ref, bre_ref, bim_ref, are_ref, aim_ref, cre_ref, cim_ref, d_ref, wg_ref,
              y_ref, hre_ref, him_ref, sre_ref, sim_ref, *, steps, chunk):
    @pl.when(pl.program_id(0) == 0)
    def _():
        hre_ref[...] = jnp.zeros_like(hre_ref)
        him_ref[...] = jnp.zeros_like(him_ref)

    u = _dot(x_ref[...].astype(BF16), win_ref[...])
    res_re, res_im = _ssm_input(u.astype(BF16), bre_ref, bim_ref)
    for j in range(SSM_N // MXU_TILE):
        cols = slice(j * MXU_TILE, (j + 1) * MXU_TILE)
        sre_ref[:, cols] = res_re[j]
        sim_ref[:, cols] = res_im[j]

    for c in range(SSM_N // chunk):
        cols = slice(c * chunk, (c + 1) * chunk)
        ar = jnp.broadcast_to(are_ref[:, cols], (SUBLANES, chunk))
        ai = jnp.broadcast_to(aim_ref[:, cols], (SUBLANES, chunk))

        def step(t, carry, cols=cols, ar=ar, ai=ai):
            hr, hi = carry
            rows = pl.ds(pl.multiple_of(t * SUBLANES, SUBLANES), SUBLANES)
            nr = ar * hr - ai * hi + sre_ref[rows, cols]
            ni = ar * hi + ai * hr + sim_ref[rows, cols]
            sre_ref[rows, cols] = nr
            sim_ref[rows, cols] = ni
            return nr, ni

        hr, hi = lax.fori_loop(0, steps, step, (hre_ref[:, cols], him_ref[:, cols]), unroll=4)
        hre_ref[:, cols] = hr
        him_ref[:, cols] = hi

    def tile(ref):
        return lambda j: ref[:, j * MXU_TILE:(j + 1) * MXU_TILE].astype(BF16)

    y = _ssm_output(tile(sre_ref), tile(sim_ref), u, cre_ref, cim_ref, d_ref, wg_ref)
    y_ref[...] = y.astype(BF16)


def _ssm_prompt(x_tm_rows, sw, bx, steps):
    assert bx == SUBLANES
    rows = steps * SUBLANES
    n = x_tm_rows.shape[0]
    weights = [sw["w_in"], sw["b_re"], sw["b_im"], sw["a_re"], sw["a_im"], sw["c_re"], sw["c_im"],
               sw["d"], sw["w_glu"]]
    state_spec = pl.BlockSpec((SUBLANES, SSM_N), lambda i: (0, 0))
    state_sds = jax.ShapeDtypeStruct((SUBLANES, SSM_N), F32)
    return pl.pallas_call(
        functools.partial(_ssm_body, steps=steps, chunk=512),
        grid=(n // rows,),
        in_specs=[pl.BlockSpec((rows, D_MODEL), lambda i: (i, 0))] + [_const_spec(w.shape) for w in weights],
        out_specs=[pl.BlockSpec((rows, SSM_W), lambda i: (i, 0)), state_spec, state_spec],
        out_shape=[jax.ShapeDtypeStruct((n, SSM_W), BF16), state_sds, state_sds],
        scratch_shapes=[pltpu.VMEM((rows, SSM_N), F32), pltpu.VMEM((rows, SSM_N), F32)],
        compiler_params=_params(("arbitrary",)),
        name="ssm_prompt",
    )(x_tm_rows, *weights)


def _decode_attn_body(q_ref, k_ref, v_ref, k0_ref, v0_ref, k1_ref, v1_ref, k2_ref, v2_ref, o_ref, lse_ref, *, bt):
    caches = ((k0_ref, v0_ref), (k1_ref, v1_ref), (k2_ref, v2_ref))
    hrow = lax.broadcasted_iota(jnp.int32, (SUBLANES, GROUP_W), 0)
    own_head = lax.broadcasted_iota(jnp.int32, (SUBLANES, GROUP_W), 1) // HEAD_DIM == hrow
    for g, (_, dil) in enumerate(DIL_GROUPS):
        kt_ref, vt_ref = caches[g]
        lb = kt_ref.shape[-1]
        cols = slice(g * GROUP_W, (g + 1) * GROUP_W)
        back = lb - lax.broadcasted_iota(jnp.int32, (SUBLANES, lb), 1)
        srow = lax.broadcasted_iota(jnp.int32, (SUBLANES, lb), 0)
        slope = jnp.zeros((SUBLANES, lb), F32)
        for h in range(HEADS):
            slope = jnp.where(srow == h, float(_ALIBI[g, h]), slope)
        valid = ((back & (dil - 1)) == 0) & (back <= N_BACK * dil)
        bias = jnp.where(valid, -slope * back.astype(F32), -jnp.inf)
        for b in range(bt):
            q, kn, vn = q_ref[b, :, cols], k_ref[b, :, cols], v_ref[b, :, cols]
            qm = jnp.where(own_head, jnp.broadcast_to(q, (SUBLANES, GROUP_W)), 0.0)
            s = _dot(qm.astype(BF16), kt_ref[b].astype(BF16)) * SM_SCALE + bias
            s_new = jnp.sum(qm * kn, axis=-1, keepdims=True) * SM_SCALE
            m = jnp.maximum(jnp.max(s, axis=-1, keepdims=True), s_new)
            p = jnp.exp(s - m)
            p_new = jnp.exp(s_new - m)
            l = jnp.sum(p, axis=-1, keepdims=True) + p_new
            o = lax.dot_general(p.astype(BF16), vt_ref[b].astype(BF16), (((1,), (1,)), ((), ())),
                                preferred_element_type=F32) + p_new * vn
            o_ref[b, :, cols] = jnp.sum(jnp.where(own_head, o / l, 0.0), axis=0, keepdims=True)
            lse_ref[b, :, cols] = jnp.sum(jnp.where(own_head, m + jnp.log(l), 0.0), axis=0, keepdims=True)


def _decode_attn(layer, q, k, v, caches, bt):
    nb = q.shape[0]
    row1 = pl.BlockSpec((bt, 1, ATT_W), lambda i: (i, 0, 0))
    views, specs = [], []
    for g, (window, dil) in enumerate(DIL_GROUPS):
        assert dil & (dil - 1) == 0
        for a in caches[2 * g:2 * g + 2]:
            lb = a.shape[2]
            assert lb == N_BACK * dil
            views.append(a.transpose(0, 1, 3, 4, 2).reshape(DEPTH, nb, GROUP_W, lb))
            specs.append(pl.BlockSpec((None, bt, GROUP_W, lb), lambda i: (layer, i, 0, 0)))
    sds = jax.ShapeDtypeStruct((nb, 1, ATT_W), F32)
    return pl.pallas_call(
        functools.partial(_decode_attn_body, bt=bt),
        grid=(nb // bt,),
        in_specs=[row1, row1, row1] + specs,
        out_specs=[row1, row1],
        out_shape=[sds, sds],
        compiler_params=_params(("parallel",)),
        name="decode_attn",
    )(q, k, v, *views)


def _decode_mix_body(up_ref, us_ref, buf_ref, h0r_ref, h0i_ref,
                     pw_ref, sc_ref, bre_ref, bim_ref, are_ref, aim_ref, cre_ref, cim_ref, d_ref, wg_ref,
                     yp_ref, ys_ref, hr_ref, hi_ref):
    up = up_ref[...]
    ys = []
    for g, w in enumerate(POOL_WINDOWS):
        gcols = slice(g * POOL_G, (g + 1) * POOL_G)
        tot = up[:, gcols]
        for i in range(POOL_BUF - (w - 1), POOL_BUF):
            tot = tot + buf_ref[i, :, gcols]
        diff = tot / float(min(PAST_LEN + 1, w)) - up[:, gcols]
        ys.append(_dot(diff.astype(BF16), pw_ref[g]))
    yp_ref[...] = (jnp.concatenate(ys, axis=-1) * sc_ref[...]).astype(BF16)

    us = us_ref[...]
    res_re, res_im = _ssm_input(us.astype(BF16), bre_ref, bim_ref)
    h_re, h_im = [], []
    for j in range(SSM_N // MXU_TILE):
        cols = slice(j * MXU_TILE, (j + 1) * MXU_TILE)
        ar, ai = are_ref[:, cols], aim_ref[:, cols]
        pr, pi = h0r_ref[:, cols], h0i_ref[:, cols]
        nr = ar * pr - ai * pi + res_re[j]
        ni = ar * pi + ai * pr + res_im[j]
        hr_ref[:, cols] = nr
        hi_ref[:, cols] = ni
        h_re.append(nr.astype(BF16))
        h_im.append(ni.astype(BF16))
    y = _ssm_output(h_re.__getitem__, h_im.__getitem__, us, cre_ref, cim_ref, d_ref, wg_ref)
    ys_ref[...] = y.astype(BF16)


def _decode_mix(layer, up, us, state_pool, h0_re, h0_im, pool_w, pool_scale, sw, bt):
    nb = up.shape[0]
    row = lambda c: pl.BlockSpec((bt, c), lambda i: (i, 0))
    buf = state_pool.transpose(0, 2, 1, 3)
    h0r = h0_re.reshape(DEPTH, nb, SSM_N)
    h0i = h0_im.reshape(DEPTH, nb, SSM_N)
    lrow = lambda c: pl.BlockSpec((None, bt, c), lambda i: (layer, i, 0))
    weights = [pool_w, pool_scale, sw["b_re"], sw["b_im"], sw["a_re"], sw["a_im"], sw["c_re"], sw["c_im"],
               sw["d"], sw["w_glu"]]
    return pl.pallas_call(
        _decode_mix_body,
        grid=(nb // bt,),
        in_specs=[row(POOL_W), row(SSM_W),
                  pl.BlockSpec((None, POOL_BUF, bt, POOL_W), lambda i: (layer, 0, i, 0)),
                  lrow(SSM_N), lrow(SSM_N)] + [_const_spec(w.shape) for w in weights],
        out_specs=[row(POOL_W), row(SSM_W), row(SSM_N), row(SSM_N)],
        out_shape=[jax.ShapeDtypeStruct((nb, POOL_W), BF16), jax.ShapeDtypeStruct((nb, SSM_W), BF16),
                   jax.ShapeDtypeStruct((nb, SSM_N), F32), jax.ShapeDtypeStruct((nb, SSM_N), F32)],
        compiler_params=_params(("parallel",)),
        name="decode_mix",
    )(up, us, buf, h0r, h0i, *weights)


def _merge_body(x_ref, o0_ref, o1_ref, o2_ref, l0_ref, l1_ref, l2_ref, yp_ref, ys_ref,
                wg_ref, wa_ref, wp_ref, ws_ref, wo_ref, g_ref, b_ref, out_ref):
    x = x_ref[...]
    xb = x.astype(BF16)
    l0, l1, l2 = l0_ref[...], l1_ref[...], l2_ref[...]
    m = jnp.maximum(jnp.maximum(l0, l1), l2)
    e0, e1, e2 = jnp.exp(l0 - m), jnp.exp(l1 - m), jnp.exp(l2 - m)
    att = (e0 * o0_ref[...] + e1 * o1_ref[...] + e2 * o2_ref[...]) / (e0 + e1 + e2)

    def gate(i):
        return jax.nn.sigmoid(_dot(xb, wg_ref[:, i * D_MODEL:(i + 1) * D_MODEL]))

    merged = gate(0) * _dot(att.astype(BF16), wa_ref[...])
    merged = merged + gate(1) * _dot(yp_ref[...], wp_ref[...])
    merged = merged + gate(2) * _dot(ys_ref[...], ws_ref[...])
    mix = _dot(merged.astype(BF16), wo_ref[...])
    out_ref[...] = _layer_norm(ALPHA * x + mix, g_ref[...], b_ref[...])


def _merge(x_tm, o_specs, o_arrays, l_arrays, yp, ys_tm, lw, bx, tm):
    lx = x_tm.shape[0]
    weights = [lw["w_gates"], lw["w_br_att"], lw["w_br_pool"], lw["w_br_ssm"], lw["w_out"],
               lw["ln1_g"], lw["ln1_b"]]
    return pl.pallas_call(
        _merge_body,
        grid=(bx, lx // tm),
        in_specs=[_tm_spec(tm, D_MODEL)] + o_specs + o_specs
                 + [_bm_spec(tm, POOL_W), _tm_spec(tm, SSM_W)] + [_const_spec(w.shape) for w in weights],
        out_specs=_tm_spec(tm, D_MODEL),
        out_shape=jax.ShapeDtypeStruct(x_tm.shape, F32),
        compiler_params=_params(("parallel", "parallel")),
        name="merge",
    )(x_tm, *o_arrays, *l_arrays, yp, ys_tm, *weights)


def _ffn_body(x_ref, wg_ref, wu_ref, wd_ref, g_ref, b_ref, out_ref):
    x = x_ref[...]
    xb = x.astype(BF16)
    f = None
    for c in range(D_FF // FF_CHUNK):
        cols = slice(c * FF_CHUNK, (c + 1) * FF_CHUNK)
        h = jax.nn.silu(_dot(xb, wg_ref[:, cols])) * _dot(xb, wu_ref[:, cols])
        part = _dot(h.astype(BF16), wd_ref[cols, :])
        f = part if f is None else f + part
    out_ref[...] = _layer_norm(ALPHA * x + f, g_ref[...], b_ref[...])


def _ffn(x_tm, lw, bx, tm, batch_major_out):
    lx = x_tm.shape[0]
    weights = [lw["ffn_w_gate"], lw["ffn_w_up"], lw["ffn_w_down"], lw["ln2_g"], lw["ln2_b"]]
    if batch_major_out:
        out_spec, out_sds = _bm_spec(tm, D_MODEL), jax.ShapeDtypeStruct((bx, lx, D_MODEL), F32)
    else:
        out_spec, out_sds = _tm_spec(tm, D_MODEL), jax.ShapeDtypeStruct(x_tm.shape, F32)
    return pl.pallas_call(
        _ffn_body,
        grid=(bx, lx // tm),
        in_specs=[_tm_spec(tm, D_MODEL)] + [_const_spec(w.shape) for w in weights],
        out_specs=out_spec,
        out_shape=out_sds,
        compiler_params=_params(("parallel", "parallel")),
        name="ffn",
    )(x_tm, *weights)


def _block_diag(blocks):
    dp, g, r, c = blocks.shape
    eye = jnp.eye(g, dtype=bool)[None, :, None, :, None]
    return jnp.where(eye, blocks[:, :, :, None, :], 0).reshape(dp, g * r, g * c)


def kernel(x_prompt, x_sample, cache_k_w128, cache_v_w128, cache_k_w512, cache_v_w512, cache_k_w2048,
           cache_v_w2048, state_pool, state_ssm_re, state_ssm_im, w_in, pool_w, pool_scale, ssm_lambda_re,
           ssm_lambda_im, ssm_b_re, ssm_b_im, ssm_c_re, ssm_c_im, ssm_d, ssm_log_step, ssm_w_glu, w_br_att,
           w_br_pool, w_br_ssm, w_out, ln1_g, ln1_b, ffn_w_gate, ffn_w_up, ffn_w_down, ln2_g, ln2_b):
    bp, lp, _ = x_prompt.shape
    nb = x_sample.shape[0]
    caches = [cache_k_w128, cache_v_w128, cache_k_w512, cache_v_w512, cache_k_w2048, cache_v_w2048]

    flat = lambda a: a.reshape(DEPTH, 1, SSM_N)
    to_lanes = lambda b: b.transpose(0, 3, 1, 2).reshape(DEPTH, SSM_CH, SSM_N)
    log_step = jnp.repeat(ssm_log_step, SSM_STATE, axis=-1)
    a_re, a_im, bb_re, bb_im = _ssm_prep(flat(ssm_lambda_re), flat(ssm_lambda_im), flat(log_step),
                                         to_lanes(ssm_b_re), to_lanes(ssm_b_im))
    from_lanes = lambda b: b.reshape(DEPTH, SSM_CH, SSM_GROUPS, SSM_STATE).transpose(0, 2, 1, 3)
    b_blk_re = _block_diag(from_lanes(bb_re)).astype(BF16)
    b_blk_im = _block_diag(from_lanes(bb_im)).astype(BF16)
    c_blk_re = _block_diag(ssm_c_re.transpose(0, 1, 3, 2)).astype(BF16)
    c_blk_im = _block_diag(ssm_c_im.transpose(0, 1, 3, 2)).astype(BF16)

    w_in_b = w_in.astype(BF16)
    c_pool, c_ssm, c_gate = QKV_W, QKV_W + POOL_W, QKV_W + POOL_W + SSM_W
    row = lambda a, l: a[l].reshape(1, -1)

    x_tm = _to_time_major(x_prompt, 512)
    xs = x_sample.reshape(nb, D_MODEL)

    st_p = [[] for _ in range(9)]
    st_s = [[] for _ in range(9)]
    y_prompt = None
    for l in range(DEPTH):
        sw = dict(w_in=w_in_b[l, :, c_pool + POOL_W:c_gate], b_re=b_blk_re[l], b_im=b_blk_im[l],
                  a_re=a_re[l], a_im=a_im[l], c_re=c_blk_re[l], c_im=c_blk_im[l],
                  d=row(ssm_d, l), w_glu=ssm_w_glu[l].astype(BF16))
        lw = dict(w_gates=w_in_b[l, :, c_gate:], w_br_att=w_br_att[l].astype(BF16),
                  w_br_pool=w_br_pool[l].astype(BF16), w_br_ssm=w_br_ssm[l].astype(BF16),
                  w_out=w_out[l].astype(BF16), ln1_g=row(ln1_g, l), ln1_b=row(ln1_b, l),
                  ffn_w_gate=ffn_w_gate[l].astype(BF16), ffn_w_up=ffn_w_up[l].astype(BF16),
                  ffn_w_down=ffn_w_down[l].astype(BF16), ln2_g=row(ln2_g, l), ln2_b=row(ln2_b, l))
        pw = pool_w[l].astype(BF16)
        psc = row(pool_scale, l)
        last = l == DEPTH - 1

        qkv = _project(x_tm, w_in_b[l, :, :QKV_W], bp, 512)
        att = [_attn_prompt(qkv, g) for g in range(N_GROUPS)]
        y_pool, pool_rows = _pool_prompt(x_tm, w_in_b[l, :, c_pool:c_ssm], pw, psc, bp, 512)
        y_ssm, h_re, h_im = _ssm_prompt(x_tm.reshape(lp * bp, D_MODEL), sw, bp, 32)
        o_specs = [_bm_spec(512, GROUP_W)] * N_GROUPS
        x1 = _merge(x_tm, o_specs, [a[0] for a in att], [a[1] for a in att], y_pool,
                    y_ssm.reshape(lp, bp * SSM_W), lw, bp, 512)
        x_next = _ffn(x1, lw, bp, 512, batch_major_out=last)
        if last:
            y_prompt = x_next
        else:
            x_tm = x_next
        for g, (window, _) in enumerate(DIL_GROUPS):
            keep = min(window, lp)
            for j in range(2):
                c0 = (1 + j) * ATT_W + g * GROUP_W
                st_p[2 * g + j].append(qkv[:, lp - keep:, c0:c0 + GROUP_W].reshape(bp, keep, HEADS, HEAD_DIM))
        st_p[6].append(pool_rows[:, POOL_HALO - POOL_BUF:])
        st_p[7].append(h_re.reshape(bp, SSM_GROUPS, SSM_STATE))
        st_p[8].append(h_im.reshape(bp, SSM_GROUPS, SSM_STATE))

        proj = _project(xs, w_in_b[l, :, :c_gate], 1, nb)[0]
        q_s, k_s, v_s = proj[:, :ATT_W], proj[:, ATT_W:2 * ATT_W], proj[:, 2 * ATT_W:QKV_W]
        up_s, us_s = proj[:, c_pool:c_ssm], proj[:, c_ssm:c_gate]
        as_rows = lambda a: a.reshape(nb, 1, ATT_W)
        o_s, lse_s = _decode_attn(l, as_rows(q_s), as_rows(k_s), as_rows(v_s), caches, 2)
        yp_s, ys_s, hr_s, hi_s = _decode_mix(l, up_s, us_s, state_pool, state_ssm_re, state_ssm_im,
                                             pw, psc, sw, nb)
        o_specs = [_bm_spec(nb, GROUP_W, col=g) for g in range(N_GROUPS)]
        o3, l3 = o_s.reshape(1, nb, ATT_W), lse_s.reshape(1, nb, ATT_W)
        x1s = _merge(xs, o_specs, [o3] * N_GROUPS, [l3] * N_GROUPS, yp_s.reshape(1, nb, POOL_W), ys_s,
                     lw, 1, nb)
        xs = _ffn(x1s, lw, 1, nb, batch_major_out=False)
        for g in range(N_GROUPS):
            cols = slice(g * GROUP_W, (g + 1) * GROUP_W)
            st_s[2 * g].append(k_s[:, cols].reshape(nb, 1, HEADS, HEAD_DIM))
            st_s[2 * g + 1].append(v_s[:, cols].reshape(nb, 1, HEADS, HEAD_DIM))
        st_s[6].append(up_s.reshape(nb, 1, POOL_W))
        st_s[7].append(hr_s.reshape(nb, SSM_GROUPS, SSM_STATE))
        st_s[8].append(hi_s.reshape(nb, SSM_GROUPS, SSM_STATE))

    y_sample = xs.reshape(nb, 1, D_MODEL)
    return (y_prompt, y_sample, *[jnp.stack(s, axis=0) for s in st_p], *[jnp.stack(s, axis=0) for s in st_s])
```

```python
import functools

import numpy as np
import jax
import jax.numpy as jnp
from jax import lax
from jax.experimental import pallas as pl
from jax.experimental.pallas import tpu as pltpu

F32 = jnp.float32
BF16 = jnp.bfloat16

D_MODEL = 1024
DEPTH = 4
PAST_LEN = 2048
HEAD_DIM = 64
HEADS = 4
DIL_GROUPS = ((128, 1), (512, 4), (2048, 16))
N_GROUPS = len(DIL_GROUPS)
N_BACK = 128
GROUP_W = HEADS * HEAD_DIM
ATT_W = N_GROUPS * GROUP_W
QKV_W = 3 * ATT_W
POOL_WINDOWS = (2, 4, 8, 16)
POOL_W = 512
POOL_G = 128
POOL_BUF = 15
POOL_HALO = 16
SSM_W = 512
SSM_GROUPS = 32
SSM_CH = 16
SSM_STATE = 64
SSM_N = SSM_GROUPS * SSM_STATE
C_POOL = QKV_W
C_SSM = C_POOL + POOL_W
C_GATE = C_SSM + SSM_W
IN_COLS = C_GATE + 3 * D_MODEL
D_FF = 2816
FF_CHUNK = 1408
ALPHA = (2 * DEPTH) ** 0.25
LN_EPS = 1e-5
SM_SCALE = HEAD_DIM ** -0.5

VMEM_LIMIT = 56 * 1024 * 1024
MXU_TILE = 256
LANES = 128
SUBLANES = 8

_ALIBI = (2.0 ** (-8.0 * np.arange(1, N_GROUPS * HEADS + 1, dtype=np.float64) / (N_GROUPS * HEADS))
          ).astype(np.float32).reshape(N_GROUPS, HEADS)


def _params(sem):
    return pltpu.CompilerParams(dimension_semantics=sem, vmem_limit_bytes=VMEM_LIMIT)


def _rows_spec(tm, c, col=0):
    return pl.BlockSpec((tm, c), lambda i: (i, col))


def _layer_spec(arr, layer):
    tail = (0,) * (arr.ndim - 1)
    return pl.BlockSpec((None,) + arr.shape[1:], lambda *_: (layer,) + tail, pipeline_mode=pl.Buffered(1))


def _dot(a, b):
    return jnp.dot(a, b, preferred_element_type=F32)


def _layer_norm(z, g, b):
    mu = jnp.mean(z, axis=-1, keepdims=True)
    zc = z - mu
    var = jnp.mean(zc * zc, axis=-1, keepdims=True)
    return zc * lax.rsqrt(var + LN_EPS) * g + b


def _proj_body(x_ref, w_ref, o_ref, *, n):
    o_ref[...] = _dot(x_ref[...].astype(BF16), w_ref[:, :n])


def _project(x, w_in, layer, n, tm):
    m = x.shape[0]
    return pl.pallas_call(
        functools.partial(_proj_body, n=n),
        grid=(m // tm,),
        in_specs=[_rows_spec(tm, D_MODEL), _layer_spec(w_in, layer)],
        out_specs=_rows_spec(tm, n),
        out_shape=jax.ShapeDtypeStruct((m, n), F32),
        compiler_params=_params(("parallel",)),
        name="project",
    )(x, w_in)


def _attn_body(q_ref, kh_ref, kc_ref, vh_ref, vc_ref, o_ref, lse_ref, *, dil, nq, slopes):
    first = pl.program_id(2) == 0
    pair = pl.program_id(1)
    nk = 2 * N_BACK
    qi = lax.broadcasted_iota(jnp.int32, (N_BACK, nk), 0)
    ki = lax.broadcasted_iota(jnp.int32, (N_BACK, nk), 1)
    back = qi + N_BACK - ki
    band = (back >= 0) & (back <= N_BACK)
    dist = (back * dil).astype(F32)
    lane = lax.broadcasted_iota(jnp.int32, (1, LANES), 1)
    heads = (lane < HEAD_DIM, lane >= HEAD_DIM)
    slope = [jnp.where(pair == 0, float(slopes[h]), float(slopes[2 + h])) for h in range(2)]
    bias = [jnp.where(band, -slope[h] * dist, -jnp.inf) for h in range(2)]
    bias_first = [jnp.where(first & (ki < N_BACK), -jnp.inf, bias[h]) for h in range(2)]

    def rows_of(j, r):
        if dil == 1:
            return pl.ds(j * N_BACK, N_BACK)
        return pl.ds(j * N_BACK * dil + r, N_BACK, stride=dil)

    def block(j, r):
        rows = rows_of(j, r)
        q = q_ref[rows, :] * SM_SCALE
        if j == 0:
            kp, vp = kh_ref[rows_of(0, r), :], vh_ref[rows_of(0, r), :]
        else:
            kp, vp = kc_ref[rows_of(j - 1, r), :], vc_ref[rows_of(j - 1, r), :]
        kk = jnp.concatenate([kp, kc_ref[rows, :]], axis=0).astype(BF16)
        vv = jnp.concatenate([vp, vc_ref[rows, :]], axis=0).astype(BF16)
        o_acc = lse_acc = None
        for h in range(2):
            qh = jnp.where(heads[h], q, 0.0).astype(BF16)
            s = lax.dot_general(qh, kk, (((1,), (1,)), ((), ())), preferred_element_type=F32)
            s = s + (bias_first[h] if j == 0 else bias[h])
            m = jnp.max(s, axis=-1, keepdims=True)
            p = jnp.exp(s - m)
            l = jnp.sum(p, axis=-1, keepdims=True)
            oh = _dot(p.astype(BF16), vv) / l
            lh = jnp.broadcast_to(m + jnp.log(l), (N_BACK, LANES))
            o_acc = oh if h == 0 else jnp.where(heads[1], oh, o_acc)
            lse_acc = lh if h == 0 else jnp.where(heads[1], lh, lse_acc)
        o_ref[rows, :] = o_acc
        lse_ref[rows, :] = lse_acc

    if nq * dil <= 4:
        for j in range(nq):
            for r in range(dil):
                block(j, r)
    else:
        assert nq == 1 and dil % 2 == 0

        def two(i, carry):
            block(0, 2 * i)
            block(0, 2 * i + 1)
            return carry

        lax.fori_loop(0, dil // 2, two, 0)


def _attn_prompt(qkv, g, nq):
    bx, lx, _ = qkv.shape
    window, dil = DIL_GROUPS[g]
    assert window // dil == N_BACK
    halo = N_BACK * dil
    tile = halo * nq
    assert lx % tile == 0
    ncol = ATT_W // LANES

    def cur(which):
        return pl.BlockSpec((None, tile, LANES), lambda b, p, n: (b, n, which * ncol + 2 * g + p))

    def before(which):
        return pl.BlockSpec((None, halo, LANES),
                            lambda b, p, n: (b, jnp.maximum(n * nq - 1, 0), which * ncol + 2 * g + p))

    out_spec = pl.BlockSpec((None, tile, LANES), lambda b, p, n: (b, n, p))
    out_sds = jax.ShapeDtypeStruct((bx, lx, GROUP_W), F32)
    return pl.pallas_call(
        functools.partial(_attn_body, dil=dil, nq=nq, slopes=_ALIBI[g]),
        grid=(bx, GROUP_W // LANES, lx // tile),
        in_specs=[cur(0), before(1), cur(1), before(2), cur(2)],
        out_specs=[out_spec, out_spec],
        out_shape=[out_sds, out_sds],
        compiler_params=_params(("parallel", "parallel", "parallel")),
        name=f"attn_prompt_g{g}",
    )(qkv, qkv, qkv, qkv, qkv)


def _pool_body(x_ref, win_ref, pw_ref, sc_ref, y_ref, st_ref, carry_ref, *, tm):
    t = pl.program_id(1)

    @pl.when(t == 0)
    def _():
        carry_ref[...] = jnp.zeros_like(carry_ref)

    u = _dot(x_ref[...].astype(BF16), win_ref[:, C_POOL:C_SSM])
    e = jnp.concatenate([carry_ref[...], u], axis=0)
    sums = []
    s = e
    for level, shift in enumerate((1, 2, 4, 8)):
        s = s[:, POOL_G:] if level else s
        s = s + pltpu.roll(s, shift, 0)
        sums.append(s[POOL_HALO:, :POOL_G])
    n_real = t * tm + 1 + lax.broadcasted_iota(jnp.int32, (tm, 1), 0)
    ys = []
    for g, w in enumerate(POOL_WINDOWS):
        cnt = jnp.minimum(n_real, w).astype(F32)
        diff = sums[g] / cnt - u[:, g * POOL_G:(g + 1) * POOL_G]
        ys.append(_dot(diff.astype(BF16), pw_ref[g]))
    y = jnp.concatenate(ys, axis=-1) * sc_ref[...]
    y_ref[...] = y.astype(BF16)
    carry_ref[...] = u[tm - POOL_HALO:, :]
    st_ref[...] = u[tm - POOL_HALO:, :]


def _pool_prompt(x3, w_in, pool_w, pool_scale, layer, tm):
    bx, lx, _ = x3.shape
    return pl.pallas_call(
        functools.partial(_pool_body, tm=tm),
        grid=(bx, lx // tm),
        in_specs=[pl.BlockSpec((None, tm, D_MODEL), lambda b, t: (b, t, 0)), _layer_spec(w_in, layer),
                  _layer_spec(pool_w, layer), _layer_spec(pool_scale, layer)],
        out_specs=[pl.BlockSpec((None, tm, POOL_W), lambda b, t: (b, t, 0)),
                   pl.BlockSpec((None, POOL_HALO, POOL_W), lambda b, t: (b, 0, 0))],
        out_shape=[jax.ShapeDtypeStruct((bx, lx, POOL_W), BF16),
                   jax.ShapeDtypeStruct((bx, POOL_HALO, POOL_W), F32)],
        scratch_shapes=[pltpu.VMEM((POOL_HALO, POOL_W), F32)],
        compiler_params=_params(("parallel", "arbitrary")),
        name="pool_prompt",
    )(x3, w_in, pool_w, pool_scale)


def _ssm_prep_body(lre_ref, lim_ref, ls_ref, bre_ref, bim_ref, are_ref, aim_ref, bbr_ref, bbi_ref):
    lr, li = lre_ref[...], lim_ref[...]
    step = jnp.exp(ls_ref[...])
    mag = jnp.exp(lr * step)
    ar = mag * jnp.cos(li * step)
    ai = mag * jnp.sin(li * step)
    den = lr * lr + li * li
    fr = ((ar - 1.0) * lr + ai * li) / den
    fi = (ai * lr - (ar - 1.0) * li) / den
    br, bi = bre_ref[...], bim_ref[...]
    are_ref[...] = ar
    aim_ref[...] = ai
    bbr_ref[...] = fr * br - fi * bi
    bbi_ref[...] = fr * bi + fi * br


def _ssm_prep(lam_re, lam_im, log_step, b_re, b_im):
    row = pl.BlockSpec((None, 1, SSM_N), lambda l: (l, 0, 0))
    mat = pl.BlockSpec((None, SSM_CH, SSM_N), lambda l: (l, 0, 0))
    row_sds = jax.ShapeDtypeStruct(lam_re.shape, F32)
    mat_sds = jax.ShapeDtypeStruct(b_re.shape, F32)
    return pl.pallas_call(
        _ssm_prep_body,
        grid=(lam_re.shape[0],),
        in_specs=[row, row, row, mat, mat],
        out_specs=[row, row, mat, mat],
        out_shape=[row_sds, row_sds, mat_sds, mat_sds],
        name="ssm_prep",
    )(lam_re, lam_im, log_step, b_re, b_im)


def _channel_slab(j):
    return j * MXU_TILE // SSM_STATE * SSM_CH // LANES


def _ssm_input(u_slab, j, bre_ref, bim_ref):
    k0 = _channel_slab(j) * LANES
    cols = slice(j * MXU_TILE, (j + 1) * MXU_TILE)
    uk = u_slab(_channel_slab(j)).astype(BF16)
    return _dot(uk, bre_ref[k0:k0 + LANES, cols]), _dot(uk, bim_ref[k0:k0 + LANES, cols])


def _ssm_output(h_re, h_im, u_slab, cre_ref, cim_ref, d_ref, wg_ref):
    n_slab = SSM_W // LANES
    acc = [None] * n_slab
    for j in range(SSM_N // MXU_TILE):
        i = _channel_slab(j)
        rows = slice(j * MXU_TILE, (j + 1) * MXU_TILE)
        cols = slice(i * LANES, (i + 1) * LANES)
        c = _dot(h_re(j), cre_ref[rows, cols]) - _dot(h_im(j), cim_ref[rows, cols])
        acc[i] = c if acc[i] is None else acc[i] + c
    y = jnp.concatenate([acc[i] + d_ref[:, i * LANES:(i + 1) * LANES] * u_slab(i) for i in range(n_slab)], axis=-1)
    y = jax.nn.gelu(y)
    return y * jax.nn.sigmoid(_dot(y.astype(BF16), wg_ref[...]))


def _ssm_body(x_ref, win_ref, bre_ref, bim_ref, are_ref, aim_ref, cre_ref, cim_ref, d_ref, wg_ref,
              y_ref, hre_ref, him_ref, u4_ref, sre_ref, sim_ref, y4_ref, *, steps, chunk):
    nbat = x_ref.shape[0]
    n_slab = SSM_W // LANES

    @pl.when(pl.program_id(0) == 0)
    def _():
        hre_ref[...] = jnp.zeros_like(hre_ref)
        him_ref[...] = jnp.zeros_like(him_ref)

    w = win_ref[:, C_SSM:C_GATE]
    for b in range(nbat):
        u_b = _dot(x_ref[b].astype(BF16), w)
        for s in range(n_slab):
            u4_ref[s, pl.ds(b, steps, stride=nbat), :] = u_b[:, s * LANES:(s + 1) * LANES]

    u_slab = lambda s: u4_ref[s]
    for j in range(SSM_N // MXU_TILE):
        cols = slice(j * MXU_TILE, (j + 1) * MXU_TILE)
        sre_ref[:, cols], sim_ref[:, cols] = _ssm_input(u_slab, j, bre_ref, bim_ref)

    for c in range(SSM_N // chunk):
        cols = slice(c * chunk, (c + 1) * chunk)
        ar = jnp.broadcast_to(are_ref[:, cols], (nbat, chunk))
        ai = jnp.broadcast_to(aim_ref[:, cols], (nbat, chunk))

        def step(t, carry, cols=cols, ar=ar, ai=ai):
            hr, hi = carry
            rows = pl.ds(pl.multiple_of(t * nbat, nbat), nbat)
            nr = ar * hr - ai * hi + sre_ref[rows, cols]
            ni = ar * hi + ai * hr + sim_ref[rows, cols]
            sre_ref[rows, cols] = nr
            sim_ref[rows, cols] = ni
            return nr, ni

        hr, hi = lax.fori_loop(0, steps, step, (hre_ref[:, cols], him_ref[:, cols]), unroll=4)
        hre_ref[:, cols] = hr
        him_ref[:, cols] = hi

    def tile(ref):
        return lambda j: ref[:, j * MXU_TILE:(j + 1) * MXU_TILE].astype(BF16)

    y = _ssm_output(tile(sre_ref), tile(sim_ref), u_slab, cre_ref, cim_ref, d_ref, wg_ref)
    for s in range(n_slab):
        y4_ref[s] = y[:, s * LANES:(s + 1) * LANES]
    for b in range(nbat):
        for s in range(n_slab):
            y_ref[b, :, s * LANES:(s + 1) * LANES] = y4_ref[s, pl.ds(b, steps, stride=nbat), :].astype(BF16)


def _ssm_prompt(x3, w_in, sw, layer, steps):
    bx, lx, _ = x3.shape
    assert bx == SUBLANES
    rows = steps * bx
    weights = [w_in, sw["b_re"], sw["b_im"], sw["a_re"], sw["a_im"], sw["c_re"], sw["c_im"], sw["d"], sw["w_glu"]]
    state_spec = pl.BlockSpec((bx, SSM_N), lambda i: (0, 0))
    state_sds = jax.ShapeDtypeStruct((bx, SSM_N), F32)
    return pl.pallas_call(
        functools.partial(_ssm_body, steps=steps, chunk=512),
        grid=(lx // steps,),
        in_specs=[pl.BlockSpec((bx, steps, D_MODEL), lambda i: (0, i, 0))] + [_layer_spec(w, layer) for w in weights],
        out_specs=[pl.BlockSpec((bx, steps, SSM_W), lambda i: (0, i, 0)), state_spec, state_spec],
        out_shape=[jax.ShapeDtypeStruct((bx, lx, SSM_W), BF16), state_sds, state_sds],
        scratch_shapes=[pltpu.VMEM((SSM_W // LANES, rows, LANES), F32), pltpu.VMEM((rows, SSM_N), F32),
                        pltpu.VMEM((rows, SSM_N), F32), pltpu.VMEM((SSM_W // LANES, rows, LANES), F32)],
        compiler_params=_params(("arbitrary",)),
        name="ssm_prompt",
    )(x3, *weights)


def _decode_attn_body(q_ref, k_ref, v_ref, k0_ref, v0_ref, k1_ref, v1_ref, k2_ref, v2_ref, o_ref, lse_ref, *, bt):
    caches = ((k0_ref, v0_ref), (k1_ref, v1_ref), (k2_ref, v2_ref))
    hrow = lax.broadcasted_iota(jnp.int32, (SUBLANES, GROUP_W), 0)
    own_head = lax.broadcasted_iota(jnp.int32, (SUBLANES, GROUP_W), 1) // HEAD_DIM == hrow
    for g, (_, dil) in enumerate(DIL_GROUPS):
        kt_ref, vt_ref = caches[g]
        lb = kt_ref.shape[-1]
        cols = slice(g * GROUP_W, (g + 1) * GROUP_W)
        back = lb - lax.broadcasted_iota(jnp.int32, (SUBLANES, lb), 1)
        srow = lax.broadcasted_iota(jnp.int32, (SUBLANES, lb), 0)
        slope = jnp.zeros((SUBLANES, lb), F32)
        for h in range(HEADS):
            slope = jnp.where(srow == h, float(_ALIBI[g, h]), slope)
        valid = ((back & (dil - 1)) == 0) & (back <= N_BACK * dil)
        bias = jnp.where(valid, -slope * back.astype(F32), -jnp.inf)
        for b in range(bt):
            q, kn, vn = q_ref[b, :, cols], k_ref[b, :, cols], v_ref[b, :, cols]
            qm = jnp.where(own_head, jnp.broadcast_to(q, (SUBLANES, GROUP_W)), 0.0)
            s = _dot(qm.astype(BF16), kt_ref[b].astype(BF16)) * SM_SCALE + bias
            s_new = jnp.sum(qm * kn, axis=-1, keepdims=True) * SM_SCALE
            m = jnp.maximum(jnp.max(s, axis=-1, keepdims=True), s_new)
            p = jnp.exp(s - m)
            p_new = jnp.exp(s_new - m)
            l = jnp.sum(p, axis=-1, keepdims=True) + p_new
            o = lax.dot_general(p.astype(BF16), vt_ref[b].astype(BF16), (((1,), (1,)), ((), ())),
                                preferred_element_type=F32) + p_new * vn
            o_ref[b, :, cols] = jnp.sum(jnp.where(own_head, o / l, 0.0), axis=0, keepdims=True)
            lse_ref[b, :, cols] = jnp.sum(jnp.where(own_head, m + jnp.log(l), 0.0), axis=0, keepdims=True)


def _decode_attn(layer, q, k, v, caches, bt):
    nb = q.shape[0]
    row1 = pl.BlockSpec((bt, 1, ATT_W), lambda i: (i, 0, 0))
    views, specs = [], []
    for g, (window, dil) in enumerate(DIL_GROUPS):
        assert dil & (dil - 1) == 0
        for a in caches[2 * g:2 * g + 2]:
            lb = a.shape[2]
            assert lb == N_BACK * dil
            views.append(a.transpose(0, 1, 3, 4, 2).reshape(DEPTH, nb, GROUP_W, lb))
            specs.append(pl.BlockSpec((None, bt, GROUP_W, lb), lambda i: (layer, i, 0, 0)))
    sds = jax.ShapeDtypeStruct((nb, 1, ATT_W), F32)
    return pl.pallas_call(
        functools.partial(_decode_attn_body, bt=bt),
        grid=(nb // bt,),
        in_specs=[row1, row1, row1] + specs,
        out_specs=[row1, row1],
        out_shape=[sds, sds],
        compiler_params=_params(("parallel",)),
        name="decode_attn",
    )(q, k, v, *views)


def _decode_mix_body(up_ref, us_ref, buf_ref, h0r_ref, h0i_ref,
                     pw_ref, sc_ref, bre_ref, bim_ref, are_ref, aim_ref, cre_ref, cim_ref, d_ref, wg_ref,
                     yp_ref, ys_ref, hr_ref, hi_ref):
    up = up_ref[...]
    ys = []
    for g, w in enumerate(POOL_WINDOWS):
        gcols = slice(g * POOL_G, (g + 1) * POOL_G)
        tot = up[:, gcols]
        for i in range(POOL_BUF - (w - 1), POOL_BUF):
            tot = tot + buf_ref[i, :, gcols]
        diff = tot / float(min(PAST_LEN + 1, w)) - up[:, gcols]
        ys.append(_dot(diff.astype(BF16), pw_ref[g]))
    yp_ref[...] = (jnp.concatenate(ys, axis=-1) * sc_ref[...]).astype(BF16)

    u_slab = lambda s: us_ref[:, s * LANES:(s + 1) * LANES]
    h_re, h_im = [], []
    for j in range(SSM_N // MXU_TILE):
        cols = slice(j * MXU_TILE, (j + 1) * MXU_TILE)
        in_re, in_im = _ssm_input(u_slab, j, bre_ref, bim_ref)
        ar, ai = are_ref[:, cols], aim_ref[:, cols]
        pr, pi = h0r_ref[:, cols], h0i_ref[:, cols]
        nr = ar * pr - ai * pi + in_re
        ni = ar * pi + ai * pr + in_im
        hr_ref[:, cols] = nr
        hi_ref[:, cols] = ni
        h_re.append(nr.astype(BF16))
        h_im.append(ni.astype(BF16))
    y = _ssm_output(h_re.__getitem__, h_im.__getitem__, u_slab, cre_ref, cim_ref, d_ref, wg_ref)
    ys_ref[...] = y.astype(BF16)


def _decode_mix(layer, up, us, state_pool, h0_re, h0_im, pool_w, pool_scale, sw, bt):
    nb = up.shape[0]
    row = lambda c: pl.BlockSpec((bt, c), lambda i: (i, 0))
    buf = state_pool.transpose(0, 2, 1, 3)
    h0r = h0_re.reshape(DEPTH, nb, SSM_N)
    h0i = h0_im.reshape(DEPTH, nb, SSM_N)
    lrow = lambda c: pl.BlockSpec((None, bt, c), lambda i: (layer, i, 0))
    weights = [pool_w, pool_scale, sw["b_re"], sw["b_im"], sw["a_re"], sw["a_im"], sw["c_re"], sw["c_im"],
               sw["d"], sw["w_glu"]]
    return pl.pallas_call(
        _decode_mix_body,
        grid=(nb // bt,),
        in_specs=[row(POOL_W), row(SSM_W),
                  pl.BlockSpec((None, POOL_BUF, bt, POOL_W), lambda i: (layer, 0, i, 0)),
                  lrow(SSM_N), lrow(SSM_N)] + [_layer_spec(w, layer) for w in weights],
        out_specs=[row(POOL_W), row(SSM_W), row(SSM_N), row(SSM_N)],
        out_shape=[jax.ShapeDtypeStruct((nb, POOL_W), BF16), jax.ShapeDtypeStruct((nb, SSM_W), BF16),
                   jax.ShapeDtypeStruct((nb, SSM_N), F32), jax.ShapeDtypeStruct((nb, SSM_N), F32)],
        compiler_params=_params(("parallel",)),
        name="decode_mix",
    )(up, us, buf, h0r, h0i, *weights)


def _merge_body(x_ref, o0_ref, o1_ref, o2_ref, l0_ref, l1_ref, l2_ref, yp_ref, ys_ref,
                win_ref, wa_ref, wp_ref, ws_ref, wo_ref, g_ref, b_ref, out_ref):
    x = x_ref[...]
    xb = x.astype(BF16)
    l0, l1, l2 = l0_ref[...], l1_ref[...], l2_ref[...]
    m = jnp.maximum(jnp.maximum(l0, l1), l2)
    e0, e1, e2 = jnp.exp(l0 - m), jnp.exp(l1 - m), jnp.exp(l2 - m)
    att = (e0 * o0_ref[...] + e1 * o1_ref[...] + e2 * o2_ref[...]) / (e0 + e1 + e2)

    def gate(i):
        return jax.nn.sigmoid(_dot(xb, win_ref[:, C_GATE + i * D_MODEL:C_GATE + (i + 1) * D_MODEL]))

    merged = gate(0) * _dot(att.astype(BF16), wa_ref[...])
    merged = merged + gate(1) * _dot(yp_ref[...], wp_ref[...])
    merged = merged + gate(2) * _dot(ys_ref[...], ws_ref[...])
    mix = _dot(merged.astype(BF16), wo_ref[...])
    out_ref[...] = _layer_norm(ALPHA * x + mix, g_ref[...], b_ref[...])


def _merge(x, o_specs, o_arrays, l_arrays, yp, ys, lw, layer, tm):
    m = x.shape[0]
    weights = [lw["w_in"], lw["w_br_att"], lw["w_br_pool"], lw["w_br_ssm"], lw["w_out"], lw["ln1_g"], lw["ln1_b"]]
    return pl.pallas_call(
        _merge_body,
        grid=(m // tm,),
        in_specs=[_rows_spec(tm, D_MODEL)] + o_specs + o_specs
                 + [_rows_spec(tm, POOL_W), _rows_spec(tm, SSM_W)] + [_layer_spec(w, layer) for w in weights],
        out_specs=_rows_spec(tm, D_MODEL),
        out_shape=jax.ShapeDtypeStruct(x.shape, F32),
        compiler_params=_params(("parallel",)),
        name="merge",
    )(x, *o_arrays, *l_arrays, yp, ys, *weights)


def _ffn_body(x_ref, wg_ref, wu_ref, wd_ref, g_ref, b_ref, out_ref):
    x = x_ref[...]
    xb = x.astype(BF16)
    f = None
    for c in range(D_FF // FF_CHUNK):
        cols = slice(c * FF_CHUNK, (c + 1) * FF_CHUNK)
        h = jax.nn.silu(_dot(xb, wg_ref[:, cols])) * _dot(xb, wu_ref[:, cols])
        part = _dot(h.astype(BF16), wd_ref[cols, :])
        f = part if f is None else f + part
    out_ref[...] = _layer_norm(ALPHA * x + f, g_ref[...], b_ref[...])


def _ffn(x, lw, layer, tm):
    m = x.shape[0]
    weights = [lw["ffn_w_gate"], lw["ffn_w_up"], lw["ffn_w_down"], lw["ln2_g"], lw["ln2_b"]]
    return pl.pallas_call(
        _ffn_body,
        grid=(m // tm,),
        in_specs=[_rows_spec(tm, D_MODEL)] + [_layer_spec(w, layer) for w in weights],
        out_specs=_rows_spec(tm, D_MODEL),
        out_shape=jax.ShapeDtypeStruct(x.shape, F32),
        compiler_params=_params(("parallel",)),
        name="ffn",
    )(x, *weights)


def _block_diag(blocks):
    dp, g, r, c = blocks.shape
    eye = jnp.eye(g, dtype=bool)[None, :, None, :, None]
    return jnp.where(eye, blocks[:, :, :, None, :], 0).reshape(dp, g * r, g * c)


def kernel(x_prompt, x_sample, cache_k_w128, cache_v_w128, cache_k_w512, cache_v_w512, cache_k_w2048,
           cache_v_w2048, state_pool, state_ssm_re, state_ssm_im, w_in, pool_w, pool_scale, ssm_lambda_re,
           ssm_lambda_im, ssm_b_re, ssm_b_im, ssm_c_re, ssm_c_im, ssm_d, ssm_log_step, ssm_w_glu, w_br_att,
           w_br_pool, w_br_ssm, w_out, ln1_g, ln1_b, ffn_w_gate, ffn_w_up, ffn_w_down, ln2_g, ln2_b):
    bp, lp, _ = x_prompt.shape
    nb = x_sample.shape[0]
    caches = [cache_k_w128, cache_v_w128, cache_k_w512, cache_v_w512, cache_k_w2048, cache_v_w2048]

    flat = lambda a: a.reshape(DEPTH, 1, SSM_N)
    to_lanes = lambda b: b.transpose(0, 3, 1, 2).reshape(DEPTH, SSM_CH, SSM_N)
    log_step = jnp.repeat(ssm_log_step, SSM_STATE, axis=-1)
    a_re, a_im, bb_re, bb_im = _ssm_prep(flat(ssm_lambda_re), flat(ssm_lambda_im), flat(log_step),
                                         to_lanes(ssm_b_re), to_lanes(ssm_b_im))
    from_lanes = lambda b: b.reshape(DEPTH, SSM_CH, SSM_GROUPS, SSM_STATE).transpose(0, 2, 1, 3)
    row = lambda a: a.reshape(DEPTH, 1, -1)
    w_in_b = w_in.astype(BF16)
    sw = dict(b_re=_block_diag(from_lanes(bb_re)).astype(BF16),
              b_im=_block_diag(from_lanes(bb_im)).astype(BF16),
              c_re=_block_diag(ssm_c_re.transpose(0, 1, 3, 2)).astype(BF16),
              c_im=_block_diag(ssm_c_im.transpose(0, 1, 3, 2)).astype(BF16),
              a_re=a_re, a_im=a_im, d=row(ssm_d), w_glu=ssm_w_glu.astype(BF16))
    lw = dict(w_in=w_in_b, w_br_att=w_br_att.astype(BF16), w_br_pool=w_br_pool.astype(BF16),
              w_br_ssm=w_br_ssm.astype(BF16), w_out=w_out.astype(BF16), ln1_g=row(ln1_g), ln1_b=row(ln1_b),
              ffn_w_gate=ffn_w_gate.astype(BF16), ffn_w_up=ffn_w_up.astype(BF16),
              ffn_w_down=ffn_w_down.astype(BF16), ln2_g=row(ln2_g), ln2_b=row(ln2_b))
    pw = pool_w.astype(BF16)
    psc = row(pool_scale)

    xp = x_prompt.reshape(bp * lp, D_MODEL)
    xs = x_sample.reshape(nb, D_MODEL)
    st_p = [[] for _ in range(9)]
    st_s = [[] for _ in range(9)]
    for l in range(DEPTH):
        x3 = xp.reshape(bp, lp, D_MODEL)
        qkv = _project(xp, w_in_b, l, QKV_W, 512).reshape(bp, lp, QKV_W)
        att = [_attn_prompt(qkv, g, nq) for g, nq in enumerate((4, 1, 1))]
        y_pool, pool_rows = _pool_prompt(x3, w_in_b, pw, psc, l, 512)
        y_ssm, h_re, h_im = _ssm_prompt(x3, w_in_b, sw, l, 64)
        flat2 = lambda a: a.reshape(bp * lp, a.shape[-1])
        x1 = _merge(xp, [_rows_spec(512, GROUP_W)] * N_GROUPS, [flat2(a[0]) for a in att],
                    [flat2(a[1]) for a in att], flat2(y_pool), flat2(y_ssm), lw, l, 512)
        xp = _ffn(x1, lw, l, 512)
        for g, (window, _) in enumerate(DIL_GROUPS):
            keep = min(window, lp)
            for j in range(2):
                c0 = (1 + j) * ATT_W + g * GROUP_W
                st_p[2 * g + j].append(qkv[:, lp - keep:, c0:c0 + GROUP_W].reshape(bp, keep, HEADS, HEAD_DIM))
        st_p[6].append(pool_rows[:, POOL_HALO - POOL_BUF:])
        st_p[7].append(h_re.reshape(bp, SSM_GROUPS, SSM_STATE))
        st_p[8].append(h_im.reshape(bp, SSM_GROUPS, SSM_STATE))

        proj = _project(xs, w_in_b, l, C_GATE, nb)
        q_s, k_s, v_s = proj[:, :ATT_W], proj[:, ATT_W:2 * ATT_W], proj[:, 2 * ATT_W:QKV_W]
        up_s, us_s = proj[:, C_POOL:C_SSM], proj[:, C_SSM:C_GATE]
        as_rows = lambda a: a.reshape(nb, 1, ATT_W)
        o_s, lse_s = _decode_attn(l, as_rows(q_s), as_rows(k_s), as_rows(v_s), caches, 2)
        yp_s, ys_s, hr_s, hi_s = _decode_mix(l, up_s, us_s, state_pool, state_ssm_re, state_ssm_im,
                                             pw, psc, sw, nb)
        o_specs = [_rows_spec(nb, GROUP_W, col=g) for g in range(N_GROUPS)]
        x1s = _merge(xs, o_specs, [o_s.reshape(nb, ATT_W)] * N_GROUPS, [lse_s.reshape(nb, ATT_W)] * N_GROUPS,
                     yp_s, ys_s, lw, l, nb)
        xs = _ffn(x1s, lw, l, nb)
        for g in range(N_GROUPS):
            cols = slice(g * GROUP_W, (g + 1) * GROUP_W)
            st_s[2 * g].append(k_s[:, cols].reshape(nb, 1, HEADS, HEAD_DIM))
            st_s[2 * g + 1].append(v_s[:, cols].reshape(nb, 1, HEADS, HEAD_DIM))
        st_s[6].append(up_s.reshape(nb, 1, POOL_W))
        st_s[7].append(hr_s.reshape(nb, SSM_GROUPS, SSM_STATE))
        st_s[8].append(hi_s.reshape(nb, SSM_GROUPS, SSM_STATE))

    return (xp.reshape(bp, lp, D_MODEL), xs.reshape(nb, 1, D_MODEL),
            *[jnp.stack(s, axis=0) for s in st_p], *[jnp.stack(s, axis=0) for s in st_s])
```

```python
import functools

import numpy as np
import jax
import jax.numpy as jnp
from jax import lax
from jax.experimental import pallas as pl
from jax.experimental.pallas import tpu as pltpu

F32 = jnp.float32
BF16 = jnp.bfloat16

D_MODEL = 1024
DEPTH = 4
PAST_LEN = 2048
HEAD_DIM = 64
HEADS = 4
DIL_GROUPS = ((128, 1), (512, 4), (2048, 16))
N_GROUPS = len(DIL_GROUPS)
N_BACK = 128
GROUP_W = HEADS * HEAD_DIM
ATT_W = N_GROUPS * GROUP_W
QKV_W = 3 * ATT_W
POOL_WINDOWS = (2, 4, 8, 16)
POOL_W = 512
POOL_G = 128
POOL_BUF = 15
POOL_HALO = 16
SSM_W = 512
SSM_GROUPS = 32
SSM_CH = 16
SSM_STATE = 64
SSM_N = SSM_GROUPS * SSM_STATE
C_POOL = QKV_W
C_SSM = C_POOL + POOL_W
C_GATE = C_SSM + SSM_W
IN_COLS = C_GATE + 3 * D_MODEL
D_FF = 2816
FF_CHUNK = 1408
ALPHA = (2 * DEPTH) ** 0.25
LN_EPS = 1e-5
SM_SCALE = HEAD_DIM ** -0.5

VMEM_LIMIT = 56 * 1024 * 1024
MXU_TILE = 256
LANES = 128
SUBLANES = 8

_ALIBI = (2.0 ** (-8.0 * np.arange(1, N_GROUPS * HEADS + 1, dtype=np.float64) / (N_GROUPS * HEADS))
          ).astype(np.float32).reshape(N_GROUPS, HEADS)


def _params(sem):
    return pltpu.CompilerParams(dimension_semantics=sem, vmem_limit_bytes=VMEM_LIMIT)


def _rows_spec(tm, c, col=0):
    return pl.BlockSpec((tm, c), lambda i: (i, col))


def _layer_spec(arr, layer):
    tail = (0,) * (arr.ndim - 1)
    return pl.BlockSpec((None,) + arr.shape[1:], lambda *_: (layer,) + tail, pipeline_mode=pl.Buffered(1))


def _dot(a, b):
    return jnp.dot(a, b, preferred_element_type=F32)


def _layer_norm(z, g, b):
    mu = jnp.mean(z, axis=-1, keepdims=True)
    zc = z - mu
    var = jnp.mean(zc * zc, axis=-1, keepdims=True)
    return zc * lax.rsqrt(var + LN_EPS) * g + b


def _proj_body(x_ref, w_ref, o_ref, *, n):
    o_ref[...] = _dot(x_ref[...].astype(BF16), w_ref[:, :n])


def _qkv_body(x_ref, w_ref, q_ref, *kv_refs):
    res = _dot(x_ref[...].astype(BF16), w_ref[:, :QKV_W])
    q_ref[...] = res[:, :ATT_W]
    for i, ref in enumerate(kv_refs):
        ref[...] = res[:, ATT_W + i * GROUP_W:ATT_W + (i + 1) * GROUP_W]


def _project_qkv(x, w_in, layer, tm):
    m = x.shape[0]
    kv_sds = jax.ShapeDtypeStruct((m, GROUP_W), F32)
    outs = pl.pallas_call(
        _qkv_body,
        grid=(m // tm,),
        in_specs=[_rows_spec(tm, D_MODEL), _layer_spec(w_in, layer)],
        out_specs=[_rows_spec(tm, ATT_W)] + [_rows_spec(tm, GROUP_W)] * (2 * N_GROUPS),
        out_shape=[jax.ShapeDtypeStruct((m, ATT_W), F32)] + [kv_sds] * (2 * N_GROUPS),
        compiler_params=_params(("parallel",)),
        name="project_qkv",
    )(x, w_in)
    return outs[0], outs[1:1 + N_GROUPS], outs[1 + N_GROUPS:]


def _project(x, w_in, layer, n, tm):
    m = x.shape[0]
    return pl.pallas_call(
        functools.partial(_proj_body, n=n),
        grid=(m // tm,),
        in_specs=[_rows_spec(tm, D_MODEL), _layer_spec(w_in, layer)],
        out_specs=_rows_spec(tm, n),
        out_shape=jax.ShapeDtypeStruct((m, n), F32),
        compiler_params=_params(("parallel",)),
        name="project",
    )(x, w_in)


def _attn_body(q_ref, kh_ref, kc_ref, vh_ref, vc_ref, o_ref, lse_ref, *, dil, nq, slopes):
    first = pl.program_id(2) == 0
    pair = pl.program_id(1)
    nk = 2 * N_BACK
    qi = lax.broadcasted_iota(jnp.int32, (N_BACK, nk), 0)
    ki = lax.broadcasted_iota(jnp.int32, (N_BACK, nk), 1)
    back = qi + N_BACK - ki
    band = (back >= 0) & (back <= N_BACK)
    dist = (back * dil).astype(F32)
    lane = lax.broadcasted_iota(jnp.int32, (1, LANES), 1)
    heads = (lane < HEAD_DIM, lane >= HEAD_DIM)
    slope = [jnp.where(pair == 0, float(slopes[h]), float(slopes[2 + h])) for h in range(2)]
    bias = [jnp.where(band, -slope[h] * dist, -jnp.inf) for h in range(2)]
    bias_first = [jnp.where(first & (ki < N_BACK), -jnp.inf, bias[h]) for h in range(2)]

    def rows_of(j, r):
        if dil == 1:
            return pl.ds(j * N_BACK, N_BACK)
        return pl.ds(j * N_BACK * dil + r, N_BACK, stride=dil)

    def block(j, r):
        rows = rows_of(j, r)
        q = q_ref[rows, :] * SM_SCALE
        if j == 0:
            kp, vp = kh_ref[rows_of(0, r), :], vh_ref[rows_of(0, r), :]
        else:
            kp, vp = kc_ref[rows_of(j - 1, r), :], vc_ref[rows_of(j - 1, r), :]
        kk = jnp.concatenate([kp, kc_ref[rows, :]], axis=0).astype(BF16)
        vv = jnp.concatenate([vp, vc_ref[rows, :]], axis=0).astype(BF16)
        o_acc = lse_acc = None
        for h in range(2):
            qh = jnp.where(heads[h], q, 0.0).astype(BF16)
            s = lax.dot_general(qh, kk, (((1,), (1,)), ((), ())), preferred_element_type=F32)
            s = s + (bias_first[h] if j == 0 else bias[h])
            m = jnp.max(s, axis=-1, keepdims=True)
            p = jnp.exp(s - m)
            l = jnp.sum(p, axis=-1, keepdims=True)
            oh = _dot(p.astype(BF16), vv) / l
            lh = jnp.broadcast_to(m + jnp.log(l), (N_BACK, LANES))
            o_acc = oh if h == 0 else jnp.where(heads[1], oh, o_acc)
            lse_acc = lh if h == 0 else jnp.where(heads[1], lh, lse_acc)
        o_ref[rows, :] = o_acc
        lse_ref[rows, :] = lse_acc

    if nq * dil <= 4:
        for j in range(nq):
            for r in range(dil):
                block(j, r)
    else:
        assert nq == 1 and dil % 2 == 0

        def two(i, carry):
            block(0, 2 * i)
            block(0, 2 * i + 1)
            return carry

        lax.fori_loop(0, dil // 2, two, 0)


def _attn_prompt(q, k, v, g, nq):
    bx, lx, _ = q.shape
    window, dil = DIL_GROUPS[g]
    assert window // dil == N_BACK
    halo = N_BACK * dil
    tile = halo * nq
    assert lx % tile == 0
    pairs = GROUP_W // LANES
    q_spec = pl.BlockSpec((None, tile, LANES), lambda b, p, n: (b, n, pairs * g + p))
    cur = pl.BlockSpec((None, tile, LANES), lambda b, p, n: (b, n, p))
    before = pl.BlockSpec((None, halo, LANES), lambda b, p, n: (b, jnp.maximum(n * nq - 1, 0), p))
    out_spec = cur
    out_sds = jax.ShapeDtypeStruct((bx, lx, GROUP_W), F32)
    return pl.pallas_call(
        functools.partial(_attn_body, dil=dil, nq=nq, slopes=_ALIBI[g]),
        grid=(bx, pairs, lx // tile),
        in_specs=[q_spec, before, cur, before, cur],
        out_specs=[out_spec, out_spec],
        out_shape=[out_sds, out_sds],
        compiler_params=_params(("parallel", "parallel", "parallel")),
        name=f"attn_prompt_g{g}",
    )(q, k, k, v, v)


def _pool_body(x_ref, win_ref, pw_ref, sc_ref, y_ref, st_ref, carry_ref, *, tm):
    t = pl.program_id(1)

    @pl.when(t == 0)
    def _():
        carry_ref[...] = jnp.zeros_like(carry_ref)

    u = _dot(x_ref[...].astype(BF16), win_ref[:, C_POOL:C_SSM])
    e = jnp.concatenate([carry_ref[...], u], axis=0)
    sums = []
    s = e
    for level, shift in enumerate((1, 2, 4, 8)):
        s = s[:, POOL_G:] if level else s
        s = s + pltpu.roll(s, shift, 0)
        sums.append(s[POOL_HALO:, :POOL_G])
    n_real = t * tm + 1 + lax.broadcasted_iota(jnp.int32, (tm, 1), 0)
    ys = []
    for g, w in enumerate(POOL_WINDOWS):
        cnt = jnp.minimum(n_real, w).astype(F32)
        diff = sums[g] / cnt - u[:, g * POOL_G:(g + 1) * POOL_G]
        ys.append(_dot(diff.astype(BF16), pw_ref[g]))
    y = jnp.concatenate(ys, axis=-1) * sc_ref[...]
    y_ref[...] = y.astype(BF16)
    carry_ref[...] = u[tm - POOL_HALO:, :]
    st_ref[...] = u[tm - POOL_HALO:, :]


def _pool_prompt(x3, w_in, pool_w, pool_scale, layer, tm):
    bx, lx, _ = x3.shape
    return pl.pallas_call(
        functools.partial(_pool_body, tm=tm),
        grid=(bx, lx // tm),
        in_specs=[pl.BlockSpec((None, tm, D_MODEL), lambda b, t: (b, t, 0)), _layer_spec(w_in, layer),
                  _layer_spec(pool_w, layer), _layer_spec(pool_scale, layer)],
        out_specs=[pl.BlockSpec((None, tm, POOL_W), lambda b, t: (b, t, 0)),
                   pl.BlockSpec((None, POOL_HALO, POOL_W), lambda b, t: (b, 0, 0))],
        out_shape=[jax.ShapeDtypeStruct((bx, lx, POOL_W), BF16),
                   jax.ShapeDtypeStruct((bx, POOL_HALO, POOL_W), F32)],
        scratch_shapes=[pltpu.VMEM((POOL_HALO, POOL_W), F32)],
        compiler_params=_params(("parallel", "arbitrary")),
        name="pool_prompt",
    )(x3, w_in, pool_w, pool_scale)


def _ssm_prep_body(lre_ref, lim_ref, ls_ref, bre_ref, bim_ref, are_ref, aim_ref, bbr_ref, bbi_ref):
    lr, li = lre_ref[...], lim_ref[...]
    step = jnp.exp(ls_ref[...])
    mag = jnp.exp(lr * step)
    ar = mag * jnp.cos(li * step)
    ai = mag * jnp.sin(li * step)
    den = lr * lr + li * li
    fr = ((ar - 1.0) * lr + ai * li) / den
    fi = (ai * lr - (ar - 1.0) * li) / den
    br, bi = bre_ref[...], bim_ref[...]
    are_ref[...] = ar
    aim_ref[...] = ai
    bbr_ref[...] = fr * br - fi * bi
    bbi_ref[...] = fr * bi + fi * br


def _ssm_prep(lam_re, lam_im, log_step, b_re, b_im):
    row = pl.BlockSpec((None, 1, SSM_N), lambda l: (l, 0, 0))
    mat = pl.BlockSpec((None, SSM_CH, SSM_N), lambda l: (l, 0, 0))
    row_sds = jax.ShapeDtypeStruct(lam_re.shape, F32)
    mat_sds = jax.ShapeDtypeStruct(b_re.shape, F32)
    return pl.pallas_call(
        _ssm_prep_body,
        grid=(lam_re.shape[0],),
        in_specs=[row, row, row, mat, mat],
        out_specs=[row, row, mat, mat],
        out_shape=[row_sds, row_sds, mat_sds, mat_sds],
        name="ssm_prep",
    )(lam_re, lam_im, log_step, b_re, b_im)


def _channel_slab(j):
    return j * MXU_TILE // SSM_STATE * SSM_CH // LANES


def _ssm_input(u_slab, j, bre_ref, bim_ref):
    uk = u_slab(_channel_slab(j)).astype(BF16)
    return _dot(uk, bre_ref[j]), _dot(uk, bim_ref[j])


def _ssm_output(h_re, h_im, u_slab, cre_ref, cim_ref, d_ref, wg_ref):
    n_slab = SSM_W // LANES
    acc = [None] * n_slab
    for j in range(SSM_N // MXU_TILE):
        i = _channel_slab(j)
        c = _dot(h_re(j), cre_ref[j]) - _dot(h_im(j), cim_ref[j])
        acc[i] = c if acc[i] is None else acc[i] + c
    y = jnp.concatenate([acc[i] + d_ref[:, i * LANES:(i + 1) * LANES] * u_slab(i) for i in range(n_slab)], axis=-1)
    y = jax.nn.gelu(y)
    return y * jax.nn.sigmoid(_dot(y.astype(BF16), wg_ref[...]))


def _ssm_body(x_ref, win_ref, bre_ref, bim_ref, are_ref, aim_ref, cre_ref, cim_ref, d_ref, wg_ref,
              y_ref, hre_ref, him_ref, u4_ref, sre_ref, sim_ref, y4_ref, *, steps, chunk):
    nbat = x_ref.shape[0]
    n_slab = SSM_W // LANES

    @pl.when(pl.program_id(0) == 0)
    def _():
        hre_ref[...] = jnp.zeros_like(hre_ref)
        him_ref[...] = jnp.zeros_like(him_ref)

    u = _dot(x_ref[...].reshape(nbat * steps, D_MODEL).astype(BF16), win_ref[:, C_SSM:C_GATE])
    for b in range(nbat):
        for s in range(n_slab):
            u4_ref[s, pl.ds(b, steps, stride=nbat), :] = u[b * steps:(b + 1) * steps, s * LANES:(s + 1) * LANES]

    u_slab = lambda s: u4_ref[s]
    for j in range(SSM_N // MXU_TILE):
        cols = slice(j * MXU_TILE, (j + 1) * MXU_TILE)
        sre_ref[:, cols], sim_ref[:, cols] = _ssm_input(u_slab, j, bre_ref, bim_ref)

    for c in range(SSM_N // chunk):
        cols = slice(c * chunk, (c + 1) * chunk)
        ar = jnp.broadcast_to(are_ref[:, cols], (nbat, chunk))
        ai = jnp.broadcast_to(aim_ref[:, cols], (nbat, chunk))

        def step(t, carry, cols=cols, ar=ar, ai=ai):
            hr, hi = carry
            rows = pl.ds(pl.multiple_of(t * nbat, nbat), nbat)
            nr = ar * hr - ai * hi + sre_ref[rows, cols]
            ni = ar * hi + ai * hr + sim_ref[rows, cols]
            sre_ref[rows, cols] = nr
            sim_ref[rows, cols] = ni
            return nr, ni

        hr, hi = lax.fori_loop(0, steps, step, (hre_ref[:, cols], him_ref[:, cols]), unroll=4)
        hre_ref[:, cols] = hr
        him_ref[:, cols] = hi

    def tile(ref):
        return lambda j: ref[:, j * MXU_TILE:(j + 1) * MXU_TILE].astype(BF16)

    y = _ssm_output(tile(sre_ref), tile(sim_ref), u_slab, cre_ref, cim_ref, d_ref, wg_ref)
    for s in range(n_slab):
        y4_ref[s] = y[:, s * LANES:(s + 1) * LANES]
    for b in range(nbat):
        for s in range(n_slab):
            y_ref[b, :, s * LANES:(s + 1) * LANES] = y4_ref[s, pl.ds(b, steps, stride=nbat), :].astype(BF16)


def _ssm_prompt(x3, w_in, sw, layer, steps):
    bx, lx, _ = x3.shape
    assert bx == SUBLANES
    rows = steps * bx
    weights = [w_in, sw["b_re"], sw["b_im"], sw["a_re"], sw["a_im"], sw["c_re"], sw["c_im"], sw["d"], sw["w_glu"]]
    state_spec = pl.BlockSpec((bx, SSM_N), lambda i: (0, 0))
    state_sds = jax.ShapeDtypeStruct((bx, SSM_N), F32)
    return pl.pallas_call(
        functools.partial(_ssm_body, steps=steps, chunk=512),
        grid=(lx // steps,),
        in_specs=[pl.BlockSpec((bx, steps, D_MODEL), lambda i: (0, i, 0))] + [_layer_spec(w, layer) for w in weights],
        out_specs=[pl.BlockSpec((bx, steps, SSM_W), lambda i: (0, i, 0)), state_spec, state_spec],
        out_shape=[jax.ShapeDtypeStruct((bx, lx, SSM_W), BF16), state_sds, state_sds],
        scratch_shapes=[pltpu.VMEM((SSM_W // LANES, rows, LANES), F32), pltpu.VMEM((rows, SSM_N), F32),
                        pltpu.VMEM((rows, SSM_N), F32), pltpu.VMEM((SSM_W // LANES, rows, LANES), F32)],
        compiler_params=_params(("arbitrary",)),
        name="ssm_prompt",
    )(x3, *weights)


def _decode_attn_body(q_ref, k_ref, v_ref, k0_ref, v0_ref, k1_ref, v1_ref, k2_ref, v2_ref, o_ref, lse_ref, *, bt):
    caches = ((k0_ref, v0_ref), (k1_ref, v1_ref), (k2_ref, v2_ref))
    hrow = lax.broadcasted_iota(jnp.int32, (SUBLANES, GROUP_W), 0)
    own_head = lax.broadcasted_iota(jnp.int32, (SUBLANES, GROUP_W), 1) // HEAD_DIM == hrow
    for g, (_, dil) in enumerate(DIL_GROUPS):
        kt_ref, vt_ref = caches[g]
        lb = kt_ref.shape[-1]
        cols = slice(g * GROUP_W, (g + 1) * GROUP_W)
        back = lb - lax.broadcasted_iota(jnp.int32, (SUBLANES, lb), 1)
        srow = lax.broadcasted_iota(jnp.int32, (SUBLANES, lb), 0)
        slope = jnp.zeros((SUBLANES, lb), F32)
        for h in range(HEADS):
            slope = jnp.where(srow == h, float(_ALIBI[g, h]), slope)
        valid = ((back & (dil - 1)) == 0) & (back <= N_BACK * dil)
        bias = jnp.where(valid, -slope * back.astype(F32), -jnp.inf)
        for b in range(bt):
            q, kn, vn = q_ref[b, :, cols], k_ref[b, :, cols], v_ref[b, :, cols]
            qm = jnp.where(own_head, jnp.broadcast_to(q, (SUBLANES, GROUP_W)), 0.0)
            s = _dot(qm.astype(BF16), kt_ref[b].astype(BF16)) * SM_SCALE + bias
            s_new = jnp.sum(qm * kn, axis=-1, keepdims=True) * SM_SCALE
            m = jnp.maximum(jnp.max(s, axis=-1, keepdims=True), s_new)
            p = jnp.exp(s - m)
            p_new = jnp.exp(s_new - m)
            l = jnp.sum(p, axis=-1, keepdims=True) + p_new
            o = lax.dot_general(p.astype(BF16), vt_ref[b].astype(BF16), (((1,), (1,)), ((), ())),
                                preferred_element_type=F32) + p_new * vn
            o_ref[b, :, cols] = jnp.sum(jnp.where(own_head, o / l, 0.0), axis=0, keepdims=True)
            lse_ref[b, :, cols] = jnp.sum(jnp.where(own_head, m + jnp.log(l), 0.0), axis=0, keepdims=True)


def _decode_attn(layer, q, k, v, caches, bt):
    nb = q.shape[0]
    row1 = pl.BlockSpec((bt, 1, ATT_W), lambda i: (i, 0, 0))
    views, specs = [], []
    for g, (window, dil) in enumerate(DIL_GROUPS):
        assert dil & (dil - 1) == 0
        for a in caches[2 * g:2 * g + 2]:
            lb = a.shape[2]
            assert lb == N_BACK * dil
            views.append(a.transpose(0, 1, 3, 4, 2).reshape(DEPTH, nb, GROUP_W, lb))
            specs.append(pl.BlockSpec((None, bt, GROUP_W, lb), lambda i: (layer, i, 0, 0)))
    sds = jax.ShapeDtypeStruct((nb, 1, ATT_W), F32)
    return pl.pallas_call(
        functools.partial(_decode_attn_body, bt=bt),
        grid=(nb // bt,),
        in_specs=[row1, row1, row1] + specs,
        out_specs=[row1, row1],
        out_shape=[sds, sds],
        compiler_params=_params(("parallel",)),
        name="decode_attn",
    )(q, k, v, *views)


def _decode_mix_body(up_ref, us_ref, buf_ref, h0r_ref, h0i_ref,
                     pw_ref, sc_ref, bre_ref, bim_ref, are_ref, aim_ref, cre_ref, cim_ref, d_ref, wg_ref,
                     yp_ref, ys_ref, hr_ref, hi_ref):
    up = up_ref[...]
    ys = []
    for g, w in enumerate(POOL_WINDOWS):
        gcols = slice(g * POOL_G, (g + 1) * POOL_G)
        tot = up[:, gcols]
        for i in range(POOL_BUF - (w - 1), POOL_BUF):
            tot = tot + buf_ref[i, :, gcols]
        diff = tot / float(min(PAST_LEN + 1, w)) - up[:, gcols]
        ys.append(_dot(diff.astype(BF16), pw_ref[g]))
    yp_ref[...] = (jnp.concatenate(ys, axis=-1) * sc_ref[...]).astype(BF16)

    u_slab = lambda s: us_ref[:, s * LANES:(s + 1) * LANES]
    h_re, h_im = [], []
    for j in range(SSM_N // MXU_TILE):
        cols = slice(j * MXU_TILE, (j + 1) * MXU_TILE)
        in_re, in_im = _ssm_input(u_slab, j, bre_ref, bim_ref)
        ar, ai = are_ref[:, cols], aim_ref[:, cols]
        pr, pi = h0r_ref[:, cols], h0i_ref[:, cols]
        nr = ar * pr - ai * pi + in_re
        ni = ar * pi + ai * pr + in_im
        hr_ref[:, cols] = nr
        hi_ref[:, cols] = ni
        h_re.append(nr.astype(BF16))
        h_im.append(ni.astype(BF16))
    y = _ssm_output(h_re.__getitem__, h_im.__getitem__, u_slab, cre_ref, cim_ref, d_ref, wg_ref)
    ys_ref[...] = y.astype(BF16)


def _decode_mix(layer, up, us, state_pool, h0_re, h0_im, pool_w, pool_scale, sw, bt):
    nb = up.shape[0]
    row = lambda c: pl.BlockSpec((bt, c), lambda i: (i, 0))
    buf = state_pool.transpose(0, 2, 1, 3)
    h0r = h0_re.reshape(DEPTH, nb, SSM_N)
    h0i = h0_im.reshape(DEPTH, nb, SSM_N)
    lrow = lambda c: pl.BlockSpec((None, bt, c), lambda i: (layer, i, 0))
    weights = [pool_w, pool_scale, sw["b_re"], sw["b_im"], sw["a_re"], sw["a_im"], sw["c_re"], sw["c_im"],
               sw["d"], sw["w_glu"]]
    return pl.pallas_call(
        _decode_mix_body,
        grid=(nb // bt,),
        in_specs=[row(POOL_W), row(SSM_W),
                  pl.BlockSpec((None, POOL_BUF, bt, POOL_W), lambda i: (layer, 0, i, 0)),
                  lrow(SSM_N), lrow(SSM_N)] + [_layer_spec(w, layer) for w in weights],
        out_specs=[row(POOL_W), row(SSM_W), row(SSM_N), row(SSM_N)],
        out_shape=[jax.ShapeDtypeStruct((nb, POOL_W), BF16), jax.ShapeDtypeStruct((nb, SSM_W), BF16),
                   jax.ShapeDtypeStruct((nb, SSM_N), F32), jax.ShapeDtypeStruct((nb, SSM_N), F32)],
        compiler_params=_params(("parallel",)),
        name="decode_mix",
    )(up, us, buf, h0r, h0i, *weights)


def _merge_body(x_ref, o0_ref, o1_ref, o2_ref, l0_ref, l1_ref, l2_ref, yp_ref, ys_ref,
                win_ref, wa_ref, wp_ref, ws_ref, wo_ref, g_ref, b_ref, out_ref):
    x = x_ref[...]
    xb = x.astype(BF16)
    l0, l1, l2 = l0_ref[...], l1_ref[...], l2_ref[...]
    m = jnp.maximum(jnp.maximum(l0, l1), l2)
    e0, e1, e2 = jnp.exp(l0 - m), jnp.exp(l1 - m), jnp.exp(l2 - m)
    att = (e0 * o0_ref[...] + e1 * o1_ref[...] + e2 * o2_ref[...]) / (e0 + e1 + e2)

    def gate(i):
        return jax.nn.sigmoid(_dot(xb, win_ref[:, C_GATE + i * D_MODEL:C_GATE + (i + 1) * D_MODEL]))

    merged = gate(0) * _dot(att.astype(BF16), wa_ref[...])
    merged = merged + gate(1) * _dot(yp_ref[...], wp_ref[...])
    merged = merged + gate(2) * _dot(ys_ref[...], ws_ref[...])
    mix = _dot(merged.astype(BF16), wo_ref[...])
    out_ref[...] = _layer_norm(ALPHA * x + mix, g_ref[...], b_ref[...])


def _merge(x, o_specs, o_arrays, l_arrays, yp, ys, lw, layer, tm):
    m = x.shape[0]
    weights = [lw["w_in"], lw["w_br_att"], lw["w_br_pool"], lw["w_br_ssm"], lw["w_out"], lw["ln1_g"], lw["ln1_b"]]
    return pl.pallas_call(
        _merge_body,
        grid=(m // tm,),
        in_specs=[_rows_spec(tm, D_MODEL)] + o_specs + o_specs
                 + [_rows_spec(tm, POOL_W), _rows_spec(tm, SSM_W)] + [_layer_spec(w, layer) for w in weights],
        out_specs=_rows_spec(tm, D_MODEL),
        out_shape=jax.ShapeDtypeStruct(x.shape, F32),
        compiler_params=_params(("parallel",)),
        name="merge",
    )(x, *o_arrays, *l_arrays, yp, ys, *weights)


def _ffn_body(x_ref, wg_ref, wu_ref, wd_ref, g_ref, b_ref, out_ref):
    x = x_ref[...]
    xb = x.astype(BF16)
    f = None
    for c in range(D_FF // FF_CHUNK):
        cols = slice(c * FF_CHUNK, (c + 1) * FF_CHUNK)
        h = jax.nn.silu(_dot(xb, wg_ref[:, cols])) * _dot(xb, wu_ref[:, cols])
        part = _dot(h.astype(BF16), wd_ref[cols, :])
        f = part if f is None else f + part
    out_ref[...] = _layer_norm(ALPHA * x + f, g_ref[...], b_ref[...])


def _ffn(x, lw, layer, tm):
    m = x.shape[0]
    weights = [lw["ffn_w_gate"], lw["ffn_w_up"], lw["ffn_w_down"], lw["ln2_g"], lw["ln2_b"]]
    return pl.pallas_call(
        _ffn_body,
        grid=(m // tm,),
        in_specs=[_rows_spec(tm, D_MODEL)] + [_layer_spec(w, layer) for w in weights],
        out_specs=_rows_spec(tm, D_MODEL),
        out_shape=jax.ShapeDtypeStruct(x.shape, F32),
        compiler_params=_params(("parallel",)),
        name="ffn",
    )(x, *weights)


def _diag_tiles(blocks, per):
    dp, g, r, c = blocks.shape
    t = blocks.reshape(dp, g // per, per, r, c)
    eye = jnp.eye(per, dtype=bool)[None, None, :, None, :, None]
    return jnp.where(eye, t[:, :, :, :, None, :], 0).reshape(dp, g // per, per * r, per * c)


def _ssm_weights(a_re, a_im, bb_re, bb_im, ssm_c_re, ssm_c_im, ssm_d, ssm_w_glu):
    per = MXU_TILE // SSM_STATE
    n_tiles = SSM_GROUPS // per
    half = (jnp.arange(n_tiles)[:, None] * (per * SSM_CH) // (LANES // 2)) % 2 == jnp.arange(2)[None, :]
    from_lanes = lambda b: b.reshape(DEPTH, SSM_CH, SSM_GROUPS, SSM_STATE).transpose(0, 2, 1, 3)

    def b_tiles(bb):
        t = _diag_tiles(from_lanes(bb), per)
        t = jnp.where(half[None, :, :, None, None], t[:, :, None], 0)
        return t.reshape(DEPTH, n_tiles, LANES, MXU_TILE).astype(BF16)

    def c_tiles(c):
        t = _diag_tiles(c.transpose(0, 1, 3, 2), per)
        t = jnp.where(half[None, :, None, :, None], t[:, :, :, None, :], 0)
        return t.reshape(DEPTH, n_tiles, MXU_TILE, LANES).astype(BF16)

    return dict(b_re=b_tiles(bb_re), b_im=b_tiles(bb_im), c_re=c_tiles(ssm_c_re), c_im=c_tiles(ssm_c_im),
                a_re=a_re, a_im=a_im, d=ssm_d.reshape(DEPTH, 1, -1), w_glu=ssm_w_glu.astype(BF16))


def kernel(x_prompt, x_sample, cache_k_w128, cache_v_w128, cache_k_w512, cache_v_w512, cache_k_w2048,
           cache_v_w2048, state_pool, state_ssm_re, state_ssm_im, w_in, pool_w, pool_scale, ssm_lambda_re,
           ssm_lambda_im, ssm_b_re, ssm_b_im, ssm_c_re, ssm_c_im, ssm_d, ssm_log_step, ssm_w_glu, w_br_att,
           w_br_pool, w_br_ssm, w_out, ln1_g, ln1_b, ffn_w_gate, ffn_w_up, ffn_w_down, ln2_g, ln2_b):
    bp, lp, _ = x_prompt.shape
    nb = x_sample.shape[0]
    caches = [cache_k_w128, cache_v_w128, cache_k_w512, cache_v_w512, cache_k_w2048, cache_v_w2048]

    flat = lambda a: a.reshape(DEPTH, 1, SSM_N)
    to_lanes = lambda b: b.transpose(0, 3, 1, 2).reshape(DEPTH, SSM_CH, SSM_N)
    log_step = jnp.repeat(ssm_log_step, SSM_STATE, axis=-1)
    a_re, a_im, bb_re, bb_im = _ssm_prep(flat(ssm_lambda_re), flat(ssm_lambda_im), flat(log_step),
                                         to_lanes(ssm_b_re), to_lanes(ssm_b_im))
    sw = _ssm_weights(a_re, a_im, bb_re, bb_im, ssm_c_re, ssm_c_im, ssm_d, ssm_w_glu)
    row = lambda a: a.reshape(DEPTH, 1, -1)
    w_in_b = w_in.astype(BF16)
    lw = dict(w_in=w_in_b, w_br_att=w_br_att.astype(BF16), w_br_pool=w_br_pool.astype(BF16),
              w_br_ssm=w_br_ssm.astype(BF16), w_out=w_out.astype(BF16), ln1_g=row(ln1_g), ln1_b=row(ln1_b),
              ffn_w_gate=ffn_w_gate.astype(BF16), ffn_w_up=ffn_w_up.astype(BF16),
              ffn_w_down=ffn_w_down.astype(BF16), ln2_g=row(ln2_g), ln2_b=row(ln2_b))
    pw = pool_w.astype(BF16)
    psc = row(pool_scale)

    xp = x_prompt.reshape(bp * lp, D_MODEL)
    xs = x_sample.reshape(nb, D_MODEL)
    st_p = [[] for _ in range(9)]
    st_s = [[] for _ in range(9)]
    for l in range(DEPTH):
        x3 = xp.reshape(bp, lp, D_MODEL)
        as3 = lambda a: a.reshape(bp, lp, a.shape[-1])
        q_p, k_p, v_p = _project_qkv(xp, w_in_b, l, 512)
        att = [_attn_prompt(as3(q_p), as3(k_p[g]), as3(v_p[g]), g, nq) for g, nq in enumerate((4, 1, 1))]
        y_pool, pool_rows = _pool_prompt(x3, w_in_b, pw, psc, l, 512)
        y_ssm, h_re, h_im = _ssm_prompt(x3, w_in_b, sw, l, 64)
        flat2 = lambda a: a.reshape(bp * lp, a.shape[-1])
        x1 = _merge(xp, [_rows_spec(512, GROUP_W)] * N_GROUPS, [flat2(a[0]) for a in att],
                    [flat2(a[1]) for a in att], flat2(y_pool), flat2(y_ssm), lw, l, 512)
        xp = _ffn(x1, lw, l, 512)
        for g, (window, _) in enumerate(DIL_GROUPS):
            keep = min(window, lp)
            for j, a in enumerate((k_p[g], v_p[g])):
                st_p[2 * g + j].append(as3(a)[:, lp - keep:].reshape(bp, keep, HEADS, HEAD_DIM))
        st_p[6].append(pool_rows[:, POOL_HALO - POOL_BUF:])
        st_p[7].append(h_re.reshape(bp, SSM_GROUPS, SSM_STATE))
        st_p[8].append(h_im.reshape(bp, SSM_GROUPS, SSM_STATE))

        proj = _project(xs, w_in_b, l, C_GATE, nb)
        q_s, k_s, v_s = proj[:, :ATT_W], proj[:, ATT_W:2 * ATT_W], proj[:, 2 * ATT_W:QKV_W]
        up_s, us_s = proj[:, C_POOL:C_SSM], proj[:, C_SSM:C_GATE]
        as_rows = lambda a: a.reshape(nb, 1, ATT_W)
        o_s, lse_s = _decode_attn(l, as_rows(q_s), as_rows(k_s), as_rows(v_s), caches, 2)
        yp_s, ys_s, hr_s, hi_s = _decode_mix(l, up_s, us_s, state_pool, state_ssm_re, state_ssm_im,
                                             pw, psc, sw, nb)
        o_specs = [_rows_spec(nb, GROUP_W, col=g) for g in range(N_GROUPS)]
        x1s = _merge(xs, o_specs, [o_s.reshape(nb, ATT_W)] * N_GROUPS, [lse_s.reshape(nb, ATT_W)] * N_GROUPS,
                     yp_s, ys_s, lw, l, nb)
        xs = _ffn(x1s, lw, l, nb)
        for g in range(N_GROUPS):
            cols = slice(g * GROUP_W, (g + 1) * GROUP_W)
            st_s[2 * g].append(k_s[:, cols].reshape(nb, 1, HEADS, HEAD_DIM))
            st_s[2 * g + 1].append(v_s[:, cols].reshape(nb, 1, HEADS, HEAD_DIM))
        st_s[6].append(up_s.reshape(nb, 1, POOL_W))
        st_s[7].append(hr_s.reshape(nb, SSM_GROUPS, SSM_STATE))
        st_s[8].append(hi_s.reshape(nb, SSM_GROUPS, SSM_STATE))

    return (xp.reshape(bp, lp, D_MODEL), xs.reshape(nb, 1, D_MODEL),
            *[jnp.stack(s, axis=0) for s in st_p], *[jnp.stack(s, axis=0) for s in st_s])
```

```python
import functools

import numpy as np
import jax
import jax.numpy as jnp
from jax import lax
from jax.experimental import pallas as pl
from jax.experimental.pallas import tpu as pltpu

F32 = jnp.float32
BF16 = jnp.bfloat16

D_MODEL = 1024
DEPTH = 4
PAST_LEN = 2048
HEAD_DIM = 64
HEADS = 4
DIL_GROUPS = ((128, 1), (512, 4), (2048, 16))
N_GROUPS = len(DIL_GROUPS)
N_BACK = 128
GROUP_W = HEADS * HEAD_DIM
ATT_W = N_GROUPS * GROUP_W
QKV_W = 3 * ATT_W
POOL_WINDOWS = (2, 4, 8, 16)
POOL_W = 512
POOL_G = 128
POOL_BUF = 15
POOL_HALO = 16
SSM_W = 512
SSM_GROUPS = 32
SSM_CH = 16
SSM_STATE = 64
SSM_N = SSM_GROUPS * SSM_STATE
C_POOL = QKV_W
C_SSM = C_POOL + POOL_W
C_GATE = C_SSM + SSM_W
IN_COLS = C_GATE + 3 * D_MODEL
D_FF = 2816
FF_CHUNK = 1408
ALPHA = (2 * DEPTH) ** 0.25
LN_EPS = 1e-5
SM_SCALE = HEAD_DIM ** -0.5

VMEM_LIMIT = 56 * 1024 * 1024
MXU_TILE = 256
LANES = 128
SUBLANES = 8

_ALIBI = (2.0 ** (-8.0 * np.arange(1, N_GROUPS * HEADS + 1, dtype=np.float64) / (N_GROUPS * HEADS))
          ).astype(np.float32).reshape(N_GROUPS, HEADS)


def _params(sem):
    return pltpu.CompilerParams(dimension_semantics=sem, vmem_limit_bytes=VMEM_LIMIT)


def _rows_spec(tm, c, col=0):
    return pl.BlockSpec((tm, c), lambda i: (i, col))


def _layer_spec(arr, layer):
    tail = (0,) * (arr.ndim - 1)
    return pl.BlockSpec((None,) + arr.shape[1:], lambda *_: (layer,) + tail, pipeline_mode=pl.Buffered(1))


def _dot(a, b):
    return jnp.dot(a, b, preferred_element_type=F32)


def _layer_norm(z, g, b):
    mu = jnp.mean(z, axis=-1, keepdims=True)
    zc = z - mu
    var = jnp.mean(zc * zc, axis=-1, keepdims=True)
    return zc * lax.rsqrt(var + LN_EPS) * g + b


def _proj_body(x_ref, w_ref, o_ref, *, n):
    o_ref[...] = _dot(x_ref[...].astype(BF16), w_ref[:, :n])


def _qkv_body(x_ref, w_ref, q_ref, *refs):
    kv_refs, (kt_ref, vt_ref) = refs[:2 * N_GROUPS], refs[2 * N_GROUPS:]
    res = _dot(x_ref[...].astype(BF16), w_ref[:, :QKV_W])
    q_ref[...] = res[:, :ATT_W]
    for i, ref in enumerate(kv_refs):
        ref[...] = res[:, ATT_W + i * GROUP_W:ATT_W + (i + 1) * GROUP_W]
    kt_ref[...] = res[:, ATT_W + (N_GROUPS - 1) * GROUP_W:ATT_W + N_GROUPS * GROUP_W].T
    vt_ref[...] = res[:, 2 * ATT_W + (N_GROUPS - 1) * GROUP_W:2 * ATT_W + N_GROUPS * GROUP_W].T


def _project_qkv(x, w_in, layer, bx, tm):
    m = x.shape[0]
    lx = m // bx
    nt = lx // tm
    kv_sds = jax.ShapeDtypeStruct((m, GROUP_W), F32)
    t_spec = pl.BlockSpec((None, GROUP_W, tm), lambda i: (i // nt, 0, i % nt))
    t_sds = jax.ShapeDtypeStruct((bx, GROUP_W, lx), F32)
    outs = pl.pallas_call(
        _qkv_body,
        grid=(m // tm,),
        in_specs=[_rows_spec(tm, D_MODEL), _layer_spec(w_in, layer)],
        out_specs=[_rows_spec(tm, ATT_W)] + [_rows_spec(tm, GROUP_W)] * (2 * N_GROUPS) + [t_spec, t_spec],
        out_shape=[jax.ShapeDtypeStruct((m, ATT_W), F32)] + [kv_sds] * (2 * N_GROUPS) + [t_sds, t_sds],
        compiler_params=_params(("parallel",)),
        name="project_qkv",
    )(x, w_in)
    return outs[0], outs[1:1 + N_GROUPS], outs[1 + N_GROUPS:1 + 2 * N_GROUPS], outs[1 + 2 * N_GROUPS:]


def _project(x, w_in, layer, n, tm):
    m = x.shape[0]
    return pl.pallas_call(
        functools.partial(_proj_body, n=n),
        grid=(m // tm,),
        in_specs=[_rows_spec(tm, D_MODEL), _layer_spec(w_in, layer)],
        out_specs=_rows_spec(tm, n),
        out_shape=jax.ShapeDtypeStruct((m, n), F32),
        compiler_params=_params(("parallel",)),
        name="project",
    )(x, w_in)


def _attn_body(q_ref, kh_ref, kc_ref, vh_ref, vc_ref, o_ref, lse_ref, *, dil, nq, slopes):
    first = pl.program_id(2) == 0
    pair = pl.program_id(1)
    nk = 2 * N_BACK
    qi = lax.broadcasted_iota(jnp.int32, (N_BACK, nk), 0)
    ki = lax.broadcasted_iota(jnp.int32, (N_BACK, nk), 1)
    back = qi + N_BACK - ki
    band = (back >= 0) & (back <= N_BACK)
    dist = (back * dil).astype(F32)
    lane = lax.broadcasted_iota(jnp.int32, (1, LANES), 1)
    second = lane >= HEAD_DIM
    slope = [jnp.where(pair == 0, float(slopes[h]), float(slopes[2 + h])) for h in range(2)]
    bias = jnp.concatenate([jnp.where(band, -slope[h] * dist, -jnp.inf) for h in range(2)], axis=0)
    ki2 = jnp.concatenate([ki, ki], axis=0)
    bias_first = jnp.where(first & (ki2 < N_BACK), -jnp.inf, bias)

    def rows_of(j, r):
        if dil == 1:
            return pl.ds(j * N_BACK, N_BACK)
        return pl.ds(j * N_BACK * dil + r, N_BACK, stride=dil)

    def block(j, r):
        rows = rows_of(j, r)
        q = q_ref[rows, :] * SM_SCALE
        if j == 0:
            kp, vp = kh_ref[rows_of(0, r), :], vh_ref[rows_of(0, r), :]
        else:
            kp, vp = kc_ref[rows_of(j - 1, r), :], vc_ref[rows_of(j - 1, r), :]
        kk = jnp.concatenate([kp, kc_ref[rows, :]], axis=0).astype(BF16)
        vv = jnp.concatenate([vp, vc_ref[rows, :]], axis=0).astype(BF16)
        q2 = jnp.concatenate([jnp.where(second, 0.0, q), jnp.where(second, q, 0.0)], axis=0).astype(BF16)
        s = lax.dot_general(q2, kk, (((1,), (1,)), ((), ())), preferred_element_type=F32)
        s = s + (bias_first if j == 0 else bias)
        m = jnp.max(s, axis=-1, keepdims=True)
        p = jnp.exp(s - m)
        l = jnp.sum(p, axis=-1, keepdims=True)
        o2 = _dot(p.astype(BF16), vv) / l
        lse2 = jnp.broadcast_to(m + jnp.log(l), (2 * N_BACK, LANES))
        o_ref[rows, :] = jnp.where(second, o2[N_BACK:], o2[:N_BACK])
        lse_ref[rows, :] = jnp.where(second, lse2[N_BACK:], lse2[:N_BACK])

    if nq * dil <= 4:
        for j in range(nq):
            for r in range(dil):
                block(j, r)
    else:
        assert nq == 1 and dil % 2 == 0

        def two(i, carry):
            block(0, 2 * i)
            block(0, 2 * i + 1)
            return carry

        lax.fori_loop(0, dil // 2, two, 0)


def _attn_prompt(q, k, v, g, nq):
    bx, lx, _ = q.shape
    window, dil = DIL_GROUPS[g]
    assert window // dil == N_BACK
    halo = N_BACK * dil
    tile = halo * nq
    assert lx % tile == 0
    pairs = GROUP_W // LANES
    q_spec = pl.BlockSpec((None, tile, LANES), lambda b, p, n: (b, n, pairs * g + p))
    cur = pl.BlockSpec((None, tile, LANES), lambda b, p, n: (b, n, p))
    before = pl.BlockSpec((None, halo, LANES), lambda b, p, n: (b, jnp.maximum(n * nq - 1, 0), p))
    out_spec = cur
    out_sds = jax.ShapeDtypeStruct((bx, lx, GROUP_W), F32)
    return pl.pallas_call(
        functools.partial(_attn_body, dil=dil, nq=nq, slopes=_ALIBI[g]),
        grid=(bx, pairs, lx // tile),
        in_specs=[q_spec, before, cur, before, cur],
        out_specs=[out_spec, out_spec],
        out_shape=[out_sds, out_sds],
        compiler_params=_params(("parallel", "parallel", "parallel")),
        name=f"attn_prompt_g{g}",
    )(q, k, k, v, v)


def _pool_body(x_ref, win_ref, pw_ref, sc_ref, y_ref, st_ref, carry_ref, *, tm):
    t = pl.program_id(1)

    @pl.when(t == 0)
    def _():
        carry_ref[...] = jnp.zeros_like(carry_ref)

    u = _dot(x_ref[...].astype(BF16), win_ref[:, C_POOL:C_SSM])
    e = jnp.concatenate([carry_ref[...], u], axis=0)
    sums = []
    s = e
    for level, shift in enumerate((1, 2, 4, 8)):
        s = s[:, POOL_G:] if level else s
        s = s + pltpu.roll(s, shift, 0)
        sums.append(s[POOL_HALO:, :POOL_G])
    n_real = t * tm + 1 + lax.broadcasted_iota(jnp.int32, (tm, 1), 0)
    ys = []
    for g, w in enumerate(POOL_WINDOWS):
        cnt = jnp.minimum(n_real, w).astype(F32)
        diff = sums[g] / cnt - u[:, g * POOL_G:(g + 1) * POOL_G]
        ys.append(_dot(diff.astype(BF16), pw_ref[g]))
    y = jnp.concatenate(ys, axis=-1) * sc_ref[...]
    y_ref[...] = y.astype(BF16)
    carry_ref[...] = u[tm - POOL_HALO:, :]
    st_ref[...] = u[tm - POOL_HALO:, :]


def _pool_prompt(x3, w_in, pool_w, pool_scale, layer, tm):
    bx, lx, _ = x3.shape
    return pl.pallas_call(
        functools.partial(_pool_body, tm=tm),
        grid=(bx, lx // tm),
        in_specs=[pl.BlockSpec((None, tm, D_MODEL), lambda b, t: (b, t, 0)), _layer_spec(w_in, layer),
                  _layer_spec(pool_w, layer), _layer_spec(pool_scale, layer)],
        out_specs=[pl.BlockSpec((None, tm, POOL_W), lambda b, t: (b, t, 0)),
                   pl.BlockSpec((None, POOL_HALO, POOL_W), lambda b, t: (b, 0, 0))],
        out_shape=[jax.ShapeDtypeStruct((bx, lx, POOL_W), BF16),
                   jax.ShapeDtypeStruct((bx, POOL_HALO, POOL_W), F32)],
        scratch_shapes=[pltpu.VMEM((POOL_HALO, POOL_W), F32)],
        compiler_params=_params(("parallel", "arbitrary")),
        name="pool_prompt",
    )(x3, w_in, pool_w, pool_scale)


def _ssm_prep_body(lre_ref, lim_ref, ls_ref, bre_ref, bim_ref, are_ref, aim_ref, bbr_ref, bbi_ref):
    lr, li = lre_ref[...], lim_ref[...]
    step = jnp.exp(ls_ref[...])
    mag = jnp.exp(lr * step)
    ar = mag * jnp.cos(li * step)
    ai = mag * jnp.sin(li * step)
    den = lr * lr + li * li
    fr = ((ar - 1.0) * lr + ai * li) / den
    fi = (ai * lr - (ar - 1.0) * li) / den
    br, bi = bre_ref[...], bim_ref[...]
    are_ref[...] = ar
    aim_ref[...] = ai
    bbr_ref[...] = fr * br - fi * bi
    bbi_ref[...] = fr * bi + fi * br


def _ssm_prep(lam_re, lam_im, log_step, b_re, b_im):
    row = pl.BlockSpec((None, 1, SSM_N), lambda l: (l, 0, 0))
    mat = pl.BlockSpec((None, SSM_CH, SSM_N), lambda l: (l, 0, 0))
    row_sds = jax.ShapeDtypeStruct(lam_re.shape, F32)
    mat_sds = jax.ShapeDtypeStruct(b_re.shape, F32)
    return pl.pallas_call(
        _ssm_prep_body,
        grid=(lam_re.shape[0],),
        in_specs=[row, row, row, mat, mat],
        out_specs=[row, row, mat, mat],
        out_shape=[row_sds, row_sds, mat_sds, mat_sds],
        name="ssm_prep",
    )(lam_re, lam_im, log_step, b_re, b_im)


def _channel_slab(j):
    return j * MXU_TILE // SSM_STATE * SSM_CH // LANES


def _ssm_input(u_slab, j, bre_ref, bim_ref):
    uk = u_slab(_channel_slab(j)).astype(BF16)
    return _dot(uk, bre_ref[j]), _dot(uk, bim_ref[j])


def _ssm_output(h_re, h_im, u_slab, cre_ref, cim_ref, d_ref, wg_ref):
    n_slab = SSM_W // LANES
    acc = [None] * n_slab
    for j in range(SSM_N // MXU_TILE):
        i = _channel_slab(j)
        c = _dot(h_re(j), cre_ref[j]) - _dot(h_im(j), cim_ref[j])
        acc[i] = c if acc[i] is None else acc[i] + c
    y = jnp.concatenate([acc[i] + d_ref[:, i * LANES:(i + 1) * LANES] * u_slab(i) for i in range(n_slab)], axis=-1)
    y = jax.nn.gelu(y)
    return y * jax.nn.sigmoid(_dot(y.astype(BF16), wg_ref[...]))


def _ssm_body(x_ref, win_ref, bre_ref, bim_ref, are_ref, aim_ref, cre_ref, cim_ref, d_ref, wg_ref,
              y_ref, hre_ref, him_ref, u4_ref, sre_ref, sim_ref, y4_ref, *, steps, chunk):
    nbat = x_ref.shape[0]
    n_slab = SSM_W // LANES

    @pl.when(pl.program_id(0) == 0)
    def _():
        hre_ref[...] = jnp.zeros_like(hre_ref)
        him_ref[...] = jnp.zeros_like(him_ref)

    u = _dot(x_ref[...].reshape(nbat * steps, D_MODEL).astype(BF16), win_ref[:, C_SSM:C_GATE])
    for b in range(nbat):
        for s in range(n_slab):
            u4_ref[s, pl.ds(b, steps, stride=nbat), :] = u[b * steps:(b + 1) * steps, s * LANES:(s + 1) * LANES]

    u_slab = lambda s: u4_ref[s]
    for j in range(SSM_N // MXU_TILE):
        cols = slice(j * MXU_TILE, (j + 1) * MXU_TILE)
        sre_ref[:, cols], sim_ref[:, cols] = _ssm_input(u_slab, j, bre_ref, bim_ref)

    for c in range(SSM_N // chunk):
        cols = slice(c * chunk, (c + 1) * chunk)
        ar = jnp.broadcast_to(are_ref[:, cols], (nbat, chunk))
        ai = jnp.broadcast_to(aim_ref[:, cols], (nbat, chunk))

        def step(t, carry, cols=cols, ar=ar, ai=ai):
            hr, hi = carry
            rows = pl.ds(pl.multiple_of(t * nbat, nbat), nbat)
            nr = ar * hr - ai * hi + sre_ref[rows, cols]
            ni = ar * hi + ai * hr + sim_ref[rows, cols]
            sre_ref[rows, cols] = nr
            sim_ref[rows, cols] = ni
            return nr, ni

        hr, hi = lax.fori_loop(0, steps, step, (hre_ref[:, cols], him_ref[:, cols]), unroll=4)
        hre_ref[:, cols] = hr
        him_ref[:, cols] = hi

    def tile(ref):
        return lambda j: ref[:, j * MXU_TILE:(j + 1) * MXU_TILE].astype(BF16)

    y = _ssm_output(tile(sre_ref), tile(sim_ref), u_slab, cre_ref, cim_ref, d_ref, wg_ref)
    for s in range(n_slab):
        y4_ref[s] = y[:, s * LANES:(s + 1) * LANES]
    for b in range(nbat):
        for s in range(n_slab):
            y_ref[b, :, s * LANES:(s + 1) * LANES] = y4_ref[s, pl.ds(b, steps, stride=nbat), :].astype(BF16)


def _ssm_prompt(x3, w_in, sw, layer, steps):
    bx, lx, _ = x3.shape
    assert bx == SUBLANES
    rows = steps * bx
    weights = [w_in, sw["b_re"], sw["b_im"], sw["a_re"], sw["a_im"], sw["c_re"], sw["c_im"], sw["d"], sw["w_glu"]]
    state_spec = pl.BlockSpec((bx, SSM_N), lambda i: (0, 0))
    state_sds = jax.ShapeDtypeStruct((bx, SSM_N), F32)
    return pl.pallas_call(
        functools.partial(_ssm_body, steps=steps, chunk=512),
        grid=(lx // steps,),
        in_specs=[pl.BlockSpec((bx, steps, D_MODEL), lambda i: (0, i, 0))] + [_layer_spec(w, layer) for w in weights],
        out_specs=[pl.BlockSpec((bx, steps, SSM_W), lambda i: (0, i, 0)), state_spec, state_spec],
        out_shape=[jax.ShapeDtypeStruct((bx, lx, SSM_W), BF16), state_sds, state_sds],
        scratch_shapes=[pltpu.VMEM((SSM_W // LANES, rows, LANES), F32), pltpu.VMEM((rows, SSM_N), F32),
                        pltpu.VMEM((rows, SSM_N), F32), pltpu.VMEM((SSM_W // LANES, rows, LANES), F32)],
        compiler_params=_params(("arbitrary",)),
        name="ssm_prompt",
    )(x3, *weights)


def _decode_attn_body(q_ref, k_ref, v_ref, k0_ref, v0_ref, k1_ref, v1_ref, k2_ref, v2_ref, o_ref, lse_ref, *, bt):
    caches = ((k0_ref, v0_ref), (k1_ref, v1_ref), (k2_ref, v2_ref))
    hrow = lax.broadcasted_iota(jnp.int32, (SUBLANES, GROUP_W), 0)
    own_head = lax.broadcasted_iota(jnp.int32, (SUBLANES, GROUP_W), 1) // HEAD_DIM == hrow
    for g, (_, dil) in enumerate(DIL_GROUPS):
        kt_ref, vt_ref = caches[g]
        lb = kt_ref.shape[-1]
        cols = slice(g * GROUP_W, (g + 1) * GROUP_W)
        back = lb - lax.broadcasted_iota(jnp.int32, (SUBLANES, lb), 1)
        srow = lax.broadcasted_iota(jnp.int32, (SUBLANES, lb), 0)
        slope = jnp.zeros((SUBLANES, lb), F32)
        for h in range(HEADS):
            slope = jnp.where(srow == h, float(_ALIBI[g, h]), slope)
        valid = ((back & (dil - 1)) == 0) & (back <= N_BACK * dil)
        bias = jnp.where(valid, -slope * back.astype(F32), -jnp.inf)
        for b in range(bt):
            q, kn, vn = q_ref[b, :, cols], k_ref[b, :, cols], v_ref[b, :, cols]
            qm = jnp.where(own_head, jnp.broadcast_to(q, (SUBLANES, GROUP_W)), 0.0)
            s = _dot(qm.astype(BF16), kt_ref[b].astype(BF16)) * SM_SCALE + bias
            s_new = jnp.sum(qm * kn, axis=-1, keepdims=True) * SM_SCALE
            m = jnp.maximum(jnp.max(s, axis=-1, keepdims=True), s_new)
            p = jnp.exp(s - m)
            p_new = jnp.exp(s_new - m)
            l = jnp.sum(p, axis=-1, keepdims=True) + p_new
            o = lax.dot_general(p.astype(BF16), vt_ref[b].astype(BF16), (((1,), (1,)), ((), ())),
                                preferred_element_type=F32) + p_new * vn
            o_ref[b, :, cols] = jnp.sum(jnp.where(own_head, o / l, 0.0), axis=0, keepdims=True)
            lse_ref[b, :, cols] = jnp.sum(jnp.where(own_head, m + jnp.log(l), 0.0), axis=0, keepdims=True)


def _decode_attn(layer, q, k, v, caches, bt):
    nb = q.shape[0]
    row1 = pl.BlockSpec((bt, 1, ATT_W), lambda i: (i, 0, 0))
    views, specs = [], []
    for g, (window, dil) in enumerate(DIL_GROUPS):
        assert dil & (dil - 1) == 0
        for a in caches[2 * g:2 * g + 2]:
            lb = a.shape[2]
            assert lb == N_BACK * dil
            views.append(a.transpose(0, 1, 3, 4, 2).reshape(DEPTH, nb, GROUP_W, lb))
            specs.append(pl.BlockSpec((None, bt, GROUP_W, lb), lambda i: (layer, i, 0, 0)))
    sds = jax.ShapeDtypeStruct((nb, 1, ATT_W), F32)
    return pl.pallas_call(
        functools.partial(_decode_attn_body, bt=bt),
        grid=(nb // bt,),
        in_specs=[row1, row1, row1] + specs,
        out_specs=[row1, row1],
        out_shape=[sds, sds],
        compiler_params=_params(("parallel",)),
        name="decode_attn",
    )(q, k, v, *views)


def _decode_mix_body(up_ref, us_ref, buf_ref, h0r_ref, h0i_ref,
                     pw_ref, sc_ref, bre_ref, bim_ref, are_ref, aim_ref, cre_ref, cim_ref, d_ref, wg_ref,
                     yp_ref, ys_ref, hr_ref, hi_ref):
    up = up_ref[...]
    ys = []
    for g, w in enumerate(POOL_WINDOWS):
        gcols = slice(g * POOL_G, (g + 1) * POOL_G)
        tot = up[:, gcols]
        for i in range(POOL_BUF - (w - 1), POOL_BUF):
            tot = tot + buf_ref[i, :, gcols]
        diff = tot / float(min(PAST_LEN + 1, w)) - up[:, gcols]
        ys.append(_dot(diff.astype(BF16), pw_ref[g]))
    yp_ref[...] = (jnp.concatenate(ys, axis=-1) * sc_ref[...]).astype(BF16)

    u_slab = lambda s: us_ref[:, s * LANES:(s + 1) * LANES]
    h_re, h_im = [], []
    for j in range(SSM_N // MXU_TILE):
        cols = slice(j * MXU_TILE, (j + 1) * MXU_TILE)
        in_re, in_im = _ssm_input(u_slab, j, bre_ref, bim_ref)
        ar, ai = are_ref[:, cols], aim_ref[:, cols]
        pr, pi = h0r_ref[:, cols], h0i_ref[:, cols]
        nr = ar * pr - ai * pi + in_re
        ni = ar * pi + ai * pr + in_im
        hr_ref[:, cols] = nr
        hi_ref[:, cols] = ni
        h_re.append(nr.astype(BF16))
        h_im.append(ni.astype(BF16))
    y = _ssm_output(h_re.__getitem__, h_im.__getitem__, u_slab, cre_ref, cim_ref, d_ref, wg_ref)
    ys_ref[...] = y.astype(BF16)


def _decode_mix(layer, up, us, state_pool, h0_re, h0_im, pool_w, pool_scale, sw, bt):
    nb = up.shape[0]
    row = lambda c: pl.BlockSpec((bt, c), lambda i: (i, 0))
    buf = state_pool.transpose(0, 2, 1, 3)
    h0r = h0_re.reshape(DEPTH, nb, SSM_N)
    h0i = h0_im.reshape(DEPTH, nb, SSM_N)
    lrow = lambda c: pl.BlockSpec((None, bt, c), lambda i: (layer, i, 0))
    weights = [pool_w, pool_scale, sw["b_re"], sw["b_im"], sw["a_re"], sw["a_im"], sw["c_re"], sw["c_im"],
               sw["d"], sw["w_glu"]]
    return pl.pallas_call(
        _decode_mix_body,
        grid=(nb // bt,),
        in_specs=[row(POOL_W), row(SSM_W),
                  pl.BlockSpec((None, POOL_BUF, bt, POOL_W), lambda i: (layer, 0, i, 0)),
                  lrow(SSM_N), lrow(SSM_N)] + [_layer_spec(w, layer) for w in weights],
        out_specs=[row(POOL_W), row(SSM_W), row(SSM_N), row(SSM_N)],
        out_shape=[jax.ShapeDtypeStruct((nb, POOL_W), BF16), jax.ShapeDtypeStruct((nb, SSM_W), BF16),
                   jax.ShapeDtypeStruct((nb, SSM_N), F32), jax.ShapeDtypeStruct((nb, SSM_N), F32)],
        compiler_params=_params(("parallel",)),
        name="decode_mix",
    )(up, us, buf, h0r, h0i, *weights)


def _merge_body(x_ref, o0_ref, o1_ref, o2_ref, l0_ref, l1_ref, l2_ref, yp_ref, ys_ref,
                win_ref, wa_ref, wp_ref, ws_ref, wo_ref, g_ref, b_ref, out_ref):
    x = x_ref[...]
    xb = x.astype(BF16)
    l0, l1, l2 = l0_ref[...], l1_ref[...], l2_ref[...]
    m = jnp.maximum(jnp.maximum(l0, l1), l2)
    e0, e1, e2 = jnp.exp(l0 - m), jnp.exp(l1 - m), jnp.exp(l2 - m)
    att = (e0 * o0_ref[...] + e1 * o1_ref[...] + e2 * o2_ref[...]) / (e0 + e1 + e2)

    def gate(i):
        return jax.nn.sigmoid(_dot(xb, win_ref[:, C_GATE + i * D_MODEL:C_GATE + (i + 1) * D_MODEL]))

    merged = gate(0) * _dot(att.astype(BF16), wa_ref[...])
    merged = merged + gate(1) * _dot(yp_ref[...], wp_ref[...])
    merged = merged + gate(2) * _dot(ys_ref[...], ws_ref[...])
    mix = _dot(merged.astype(BF16), wo_ref[...])
    out_ref[...] = _layer_norm(ALPHA * x + mix, g_ref[...], b_ref[...])


def _merge(x, o_specs, o_arrays, l_arrays, yp, ys, lw, layer, tm):
    m = x.shape[0]
    weights = [lw["w_in"], lw["w_br_att"], lw["w_br_pool"], lw["w_br_ssm"], lw["w_out"], lw["ln1_g"], lw["ln1_b"]]
    return pl.pallas_call(
        _merge_body,
        grid=(m // tm,),
        in_specs=[_rows_spec(tm, D_MODEL)] + o_specs + o_specs
                 + [_rows_spec(tm, POOL_W), _rows_spec(tm, SSM_W)] + [_layer_spec(w, layer) for w in weights],
        out_specs=_rows_spec(tm, D_MODEL),
        out_shape=jax.ShapeDtypeStruct(x.shape, F32),
        compiler_params=_params(("parallel",)),
        name="merge",
    )(x, *o_arrays, *l_arrays, yp, ys, *weights)


def _ffn_body(x_ref, wg_ref, wu_ref, wd_ref, g_ref, b_ref, out_ref):
    x = x_ref[...]
    xb = x.astype(BF16)
    f = None
    for c in range(D_FF // FF_CHUNK):
        cols = slice(c * FF_CHUNK, (c + 1) * FF_CHUNK)
        h = jax.nn.silu(_dot(xb, wg_ref[:, cols])) * _dot(xb, wu_ref[:, cols])
        part = _dot(h.astype(BF16), wd_ref[cols, :])
        f = part if f is None else f + part
    out_ref[...] = _layer_norm(ALPHA * x + f, g_ref[...], b_ref[...])


def _ffn(x, lw, layer, tm):
    m = x.shape[0]
    weights = [lw["ffn_w_gate"], lw["ffn_w_up"], lw["ffn_w_down"], lw["ln2_g"], lw["ln2_b"]]
    return pl.pallas_call(
        _ffn_body,
        grid=(m // tm,),
        in_specs=[_rows_spec(tm, D_MODEL)] + [_layer_spec(w, layer) for w in weights],
        out_specs=_rows_spec(tm, D_MODEL),
        out_shape=jax.ShapeDtypeStruct(x.shape, F32),
        compiler_params=_params(("parallel",)),
        name="ffn",
    )(x, *weights)


def _diag_tiles(blocks, per):
    dp, g, r, c = blocks.shape
    t = blocks.reshape(dp, g // per, per, r, c)
    eye = jnp.eye(per, dtype=bool)[None, None, :, None, :, None]
    return jnp.where(eye, t[:, :, :, :, None, :], 0).reshape(dp, g // per, per * r, per * c)


def _ssm_weights(a_re, a_im, bb_re, bb_im, ssm_c_re, ssm_c_im, ssm_d, ssm_w_glu):
    per = MXU_TILE // SSM_STATE
    n_tiles = SSM_GROUPS // per
    half = (jnp.arange(n_tiles)[:, None] * (per * SSM_CH) // (LANES // 2)) % 2 == jnp.arange(2)[None, :]
    from_lanes = lambda b: b.reshape(DEPTH, SSM_CH, SSM_GROUPS, SSM_STATE).transpose(0, 2, 1, 3)

    def b_tiles(bb):
        t = _diag_tiles(from_lanes(bb), per)
        t = jnp.where(half[None, :, :, None, None], t[:, :, None], 0)
        return t.reshape(DEPTH, n_tiles, LANES, MXU_TILE).astype(BF16)

    def c_tiles(c):
        t = _diag_tiles(c.transpose(0, 1, 3, 2), per)
        t = jnp.where(half[None, :, None, :, None], t[:, :, :, None, :], 0)
        return t.reshape(DEPTH, n_tiles, MXU_TILE, LANES).astype(BF16)

    return dict(b_re=b_tiles(bb_re), b_im=b_tiles(bb_im), c_re=c_tiles(ssm_c_re), c_im=c_tiles(ssm_c_im),
                a_re=a_re, a_im=a_im, d=ssm_d.reshape(DEPTH, 1, -1), w_glu=ssm_w_glu.astype(BF16))


def kernel(x_prompt, x_sample, cache_k_w128, cache_v_w128, cache_k_w512, cache_v_w512, cache_k_w2048,
           cache_v_w2048, state_pool, state_ssm_re, state_ssm_im, w_in, pool_w, pool_scale, ssm_lambda_re,
           ssm_lambda_im, ssm_b_re, ssm_b_im, ssm_c_re, ssm_c_im, ssm_d, ssm_log_step, ssm_w_glu, w_br_att,
           w_br_pool, w_br_ssm, w_out, ln1_g, ln1_b, ffn_w_gate, ffn_w_up, ffn_w_down, ln2_g, ln2_b):
    bp, lp, _ = x_prompt.shape
    nb = x_sample.shape[0]
    caches = [cache_k_w128, cache_v_w128, cache_k_w512, cache_v_w512, cache_k_w2048, cache_v_w2048]

    flat = lambda a: a.reshape(DEPTH, 1, SSM_N)
    to_lanes = lambda b: b.transpose(0, 3, 1, 2).reshape(DEPTH, SSM_CH, SSM_N)
    log_step = jnp.repeat(ssm_log_step, SSM_STATE, axis=-1)
    a_re, a_im, bb_re, bb_im = _ssm_prep(flat(ssm_lambda_re), flat(ssm_lambda_im), flat(log_step),
                                         to_lanes(ssm_b_re), to_lanes(ssm_b_im))
    sw = _ssm_weights(a_re, a_im, bb_re, bb_im, ssm_c_re, ssm_c_im, ssm_d, ssm_w_glu)
    row = lambda a: a.reshape(DEPTH, 1, -1)
    w_in_b = w_in.astype(BF16)
    lw = dict(w_in=w_in_b, w_br_att=w_br_att.astype(BF16), w_br_pool=w_br_pool.astype(BF16),
              w_br_ssm=w_br_ssm.astype(BF16), w_out=w_out.astype(BF16), ln1_g=row(ln1_g), ln1_b=row(ln1_b),
              ffn_w_gate=ffn_w_gate.astype(BF16), ffn_w_up=ffn_w_up.astype(BF16),
              ffn_w_down=ffn_w_down.astype(BF16), ln2_g=row(ln2_g), ln2_b=row(ln2_b))
    pw = pool_w.astype(BF16)
    psc = row(pool_scale)

    xp = x_prompt.reshape(bp * lp, D_MODEL)
    xs = x_sample.reshape(nb, D_MODEL)
    st_p = [[] for _ in range(9)]
    st_s = [[] for _ in range(9)]
    for l in range(DEPTH):
        x3 = xp.reshape(bp, lp, D_MODEL)
        as3 = lambda a: a.reshape(bp, lp, a.shape[-1])
        q_p, k_p, v_p, kv_t = _project_qkv(xp, w_in_b, l, bp, 512)
        att = [_attn_prompt(as3(q_p), as3(k_p[g]), as3(v_p[g]), g, nq) for g, nq in enumerate((4, 1, 1))]
        y_pool, pool_rows = _pool_prompt(x3, w_in_b, pw, psc, l, 512)
        y_ssm, h_re, h_im = _ssm_prompt(x3, w_in_b, sw, l, 64)
        flat2 = lambda a: a.reshape(bp * lp, a.shape[-1])
        x1 = _merge(xp, [_rows_spec(512, GROUP_W)] * N_GROUPS, [flat2(a[0]) for a in att],
                    [flat2(a[1]) for a in att], flat2(y_pool), flat2(y_ssm), lw, l, 512)
        xp = _ffn(x1, lw, l, 512)
        for g, (window, _) in enumerate(DIL_GROUPS):
            keep = min(window, lp)
            for j, a in enumerate((k_p[g], v_p[g])):
                if g == N_GROUPS - 1 and keep == lp:
                    st_p[2 * g + j].append(kv_t[j].reshape(bp, HEADS, HEAD_DIM, lp).transpose(0, 3, 1, 2))
                else:
                    st_p[2 * g + j].append(as3(a)[:, lp - keep:].reshape(bp, keep, HEADS, HEAD_DIM))
        st_p[6].append(pool_rows[:, POOL_HALO - POOL_BUF:])
        st_p[7].append(h_re.reshape(bp, SSM_GROUPS, SSM_STATE))
        st_p[8].append(h_im.reshape(bp, SSM_GROUPS, SSM_STATE))

        proj = _project(xs, w_in_b, l, C_GATE, nb)
        q_s, k_s, v_s = proj[:, :ATT_W], proj[:, ATT_W:2 * ATT_W], proj[:, 2 * ATT_W:QKV_W]
        up_s, us_s = proj[:, C_POOL:C_SSM], proj[:, C_SSM:C_GATE]
        as_rows = lambda a: a.reshape(nb, 1, ATT_W)
        o_s, lse_s = _decode_attn(l, as_rows(q_s), as_rows(k_s), as_rows(v_s), caches, 2)
        yp_s, ys_s, hr_s, hi_s = _decode_mix(l, up_s, us_s, state_pool, state_ssm_re, state_ssm_im,
                                             pw, psc, sw, nb)
        o_specs = [_rows_spec(nb, GROUP_W, col=g) for g in range(N_GROUPS)]
        x1s = _merge(xs, o_specs, [o_s.reshape(nb, ATT_W)] * N_GROUPS, [lse_s.reshape(nb, ATT_W)] * N_GROUPS,
                     yp_s, ys_s, lw, l, nb)
        xs = _ffn(x1s, lw, l, nb)
        for g in range(N_GROUPS):
            cols = slice(g * GROUP_W, (g + 1) * GROUP_W)
            st_s[2 * g].append(k_s[:, cols].reshape(nb, 1, HEADS, HEAD_DIM))
            st_s[2 * g + 1].append(v_s[:, cols].reshape(nb, 1, HEADS, HEAD_DIM))
        st_s[6].append(up_s.reshape(nb, 1, POOL_W))
        st_s[7].append(hr_s.reshape(nb, SSM_GROUPS, SSM_STATE))
        st_s[8].append(hi_s.reshape(nb, SSM_GROUPS, SSM_STATE))

    return (xp.reshape(bp, lp, D_MODEL), xs.reshape(nb, 1, D_MODEL),
            *[jnp.stack(s, axis=0) for s in st_p], *[jnp.stack(s, axis=0) for s in st_s])
```

```python
import functools

import numpy as np
import jax
import jax.numpy as jnp
from jax import lax
from jax.experimental import pallas as pl
from jax.experimental.pallas import tpu as pltpu

F32 = jnp.float32
BF16 = jnp.bfloat16

D_MODEL = 1024
DEPTH = 4
PAST_LEN = 2048
HEAD_DIM = 64
HEADS = 4
DIL_GROUPS = ((128, 1), (512, 4), (2048, 16))
N_GROUPS = len(DIL_GROUPS)
N_BACK = 128
GROUP_W = HEADS * HEAD_DIM
ATT_W = N_GROUPS * GROUP_W
QKV_W = 3 * ATT_W
POOL_WINDOWS = (2, 4, 8, 16)
POOL_W = 512
POOL_G = 128
POOL_BUF = 15
POOL_HALO = 16
SSM_W = 512
SSM_GROUPS = 32
SSM_CH = 16
SSM_STATE = 64
SSM_N = SSM_GROUPS * SSM_STATE
C_POOL = QKV_W
C_SSM = C_POOL + POOL_W
C_GATE = C_SSM + SSM_W
IN_COLS = C_GATE + 3 * D_MODEL
D_FF = 2816
FF_CHUNK = 1408
ALPHA = (2 * DEPTH) ** 0.25
LN_EPS = 1e-5
SM_SCALE = HEAD_DIM ** -0.5

VMEM_LIMIT = 56 * 1024 * 1024
MXU_TILE = 256
LANES = 128
SUBLANES = 8

_ALIBI = (2.0 ** (-8.0 * np.arange(1, N_GROUPS * HEADS + 1, dtype=np.float64) / (N_GROUPS * HEADS))
          ).astype(np.float32).reshape(N_GROUPS, HEADS)


def _params(sem):
    return pltpu.CompilerParams(dimension_semantics=sem, vmem_limit_bytes=VMEM_LIMIT)


def _rows_spec(tm, c, col=0):
    return pl.BlockSpec((tm, c), lambda i: (i, col))


def _layer_spec(arr, layer):
    tail = (0,) * (arr.ndim - 1)
    return pl.BlockSpec((None,) + arr.shape[1:], lambda *_: (layer,) + tail, pipeline_mode=pl.Buffered(1))


def _dot(a, b):
    return jnp.dot(a, b, preferred_element_type=F32)


def _layer_norm(z, g, b):
    mu = jnp.mean(z, axis=-1, keepdims=True)
    zc = z - mu
    var = jnp.mean(zc * zc, axis=-1, keepdims=True)
    return zc * lax.rsqrt(var + LN_EPS) * g + b


def _proj_body(x_ref, w_ref, o_ref, *, n):
    o_ref[...] = _dot(x_ref[...].astype(BF16), w_ref[:, :n])


def _qkv_body(x_ref, w_ref, q_ref, *refs):
    kv_refs, (kt_ref, vt_ref) = refs[:2 * N_GROUPS], refs[2 * N_GROUPS:]
    res = _dot(x_ref[...].astype(BF16), w_ref[:, :QKV_W])
    q_ref[...] = res[:, :ATT_W]
    for i, ref in enumerate(kv_refs):
        ref[...] = res[:, ATT_W + i * GROUP_W:ATT_W + (i + 1) * GROUP_W]
    kt_ref[...] = res[:, ATT_W + (N_GROUPS - 1) * GROUP_W:ATT_W + N_GROUPS * GROUP_W].T
    vt_ref[...] = res[:, 2 * ATT_W + (N_GROUPS - 1) * GROUP_W:2 * ATT_W + N_GROUPS * GROUP_W].T


def _project_qkv(x, w_in, layer, bx, tm):
    m = x.shape[0]
    lx = m // bx
    nt = lx // tm
    kv_sds = jax.ShapeDtypeStruct((m, GROUP_W), F32)
    t_spec = pl.BlockSpec((None, GROUP_W, tm), lambda i: (i // nt, 0, i % nt))
    t_sds = jax.ShapeDtypeStruct((bx, GROUP_W, lx), F32)
    outs = pl.pallas_call(
        _qkv_body,
        grid=(m // tm,),
        in_specs=[_rows_spec(tm, D_MODEL), _layer_spec(w_in, layer)],
        out_specs=[_rows_spec(tm, ATT_W)] + [_rows_spec(tm, GROUP_W)] * (2 * N_GROUPS) + [t_spec, t_spec],
        out_shape=[jax.ShapeDtypeStruct((m, ATT_W), F32)] + [kv_sds] * (2 * N_GROUPS) + [t_sds, t_sds],
        compiler_params=_params(("parallel",)),
        name="project_qkv",
    )(x, w_in)
    return outs[0], outs[1:1 + N_GROUPS], outs[1 + N_GROUPS:1 + 2 * N_GROUPS], outs[1 + 2 * N_GROUPS:]


def _project(x, w_in, layer, n, tm):
    m = x.shape[0]
    return pl.pallas_call(
        functools.partial(_proj_body, n=n),
        grid=(m // tm,),
        in_specs=[_rows_spec(tm, D_MODEL), _layer_spec(w_in, layer)],
        out_specs=_rows_spec(tm, n),
        out_shape=jax.ShapeDtypeStruct((m, n), F32),
        compiler_params=_params(("parallel",)),
        name="project",
    )(x, w_in)


def _attn_body(q_ref, kh_ref, kc_ref, vh_ref, vc_ref, o_ref, lse_ref, *, dil, nq, slopes):
    first = pl.program_id(2) == 0
    pair = pl.program_id(1)
    nk = 2 * N_BACK
    qi = lax.broadcasted_iota(jnp.int32, (N_BACK, nk), 0)
    ki = lax.broadcasted_iota(jnp.int32, (N_BACK, nk), 1)
    back = qi + N_BACK - ki
    band = (back >= 0) & (back <= N_BACK)
    dist = (back * dil).astype(F32)
    lane = lax.broadcasted_iota(jnp.int32, (1, LANES), 1)
    second = lane >= HEAD_DIM
    slope = [jnp.where(pair == 0, float(slopes[h]), float(slopes[2 + h])) for h in range(2)]
    bias = jnp.concatenate([jnp.where(band, -slope[h] * dist, -jnp.inf) for h in range(2)], axis=0)
    ki2 = jnp.concatenate([ki, ki], axis=0)
    bias_first = jnp.where(first & (ki2 < N_BACK), -jnp.inf, bias)

    def rows_of(j, r):
        if dil == 1:
            return pl.ds(j * N_BACK, N_BACK)
        return pl.ds(j * N_BACK * dil + r, N_BACK, stride=dil)

    def block(j, r):
        rows = rows_of(j, r)
        q = q_ref[rows, :] * SM_SCALE
        if j == 0:
            kp, vp = kh_ref[rows_of(0, r), :], vh_ref[rows_of(0, r), :]
        else:
            kp, vp = kc_ref[rows_of(j - 1, r), :], vc_ref[rows_of(j - 1, r), :]
        kk = jnp.concatenate([kp, kc_ref[rows, :]], axis=0).astype(BF16)
        vv = jnp.concatenate([vp, vc_ref[rows, :]], axis=0).astype(BF16)
        q2 = jnp.concatenate([jnp.where(second, 0.0, q), jnp.where(second, q, 0.0)], axis=0).astype(BF16)
        s = lax.dot_general(q2, kk, (((1,), (1,)), ((), ())), preferred_element_type=F32)
        s = s + (bias_first if j == 0 else bias)
        m = jnp.max(s, axis=-1, keepdims=True)
        p = jnp.exp(s - m)
        l = jnp.sum(p, axis=-1, keepdims=True)
        o2 = _dot(p.astype(BF16), vv) / l
        lse2 = jnp.broadcast_to(m + jnp.log(l), (2 * N_BACK, LANES))
        o_ref[rows, :] = jnp.where(second, o2[N_BACK:], o2[:N_BACK])
        lse_ref[rows, :] = jnp.where(second, lse2[N_BACK:], lse2[:N_BACK])

    if nq * dil <= 4:
        for j in range(nq):
            for r in range(dil):
                block(j, r)
    else:
        per_iter = 4
        assert nq == 1 and dil % per_iter == 0

        def several(i, carry):
            for k in range(per_iter):
                block(0, per_iter * i + k)
            return carry

        lax.fori_loop(0, dil // per_iter, several, 0)


def _attn_prompt(q, k, v, g, nq):
    bx, lx, _ = q.shape
    window, dil = DIL_GROUPS[g]
    assert window // dil == N_BACK
    halo = N_BACK * dil
    tile = halo * nq
    assert lx % tile == 0
    pairs = GROUP_W // LANES
    q_spec = pl.BlockSpec((None, tile, LANES), lambda b, p, n: (b, n, pairs * g + p))
    cur = pl.BlockSpec((None, tile, LANES), lambda b, p, n: (b, n, p))
    before = pl.BlockSpec((None, halo, LANES), lambda b, p, n: (b, jnp.maximum(n * nq - 1, 0), p))
    out_spec = cur
    out_sds = jax.ShapeDtypeStruct((bx, lx, GROUP_W), F32)
    return pl.pallas_call(
        functools.partial(_attn_body, dil=dil, nq=nq, slopes=_ALIBI[g]),
        grid=(bx, pairs, lx // tile),
        in_specs=[q_spec, before, cur, before, cur],
        out_specs=[out_spec, out_spec],
        out_shape=[out_sds, out_sds],
        compiler_params=_params(("parallel", "parallel", "parallel")),
        name=f"attn_prompt_g{g}",
    )(q, k, k, v, v)


def _pool_body(x_ref, win_ref, pw_ref, sc_ref, y_ref, st_ref, carry_ref, *, tm):
    t = pl.program_id(1)

    @pl.when(t == 0)
    def _():
        carry_ref[...] = jnp.zeros_like(carry_ref)

    u = _dot(x_ref[...].astype(BF16), win_ref[:, C_POOL:C_SSM])
    e = jnp.concatenate([carry_ref[...], u], axis=0)
    sums = []
    s = e
    for level, shift in enumerate((1, 2, 4, 8)):
        s = s[:, POOL_G:] if level else s
        s = s + pltpu.roll(s, shift, 0)
        sums.append(s[POOL_HALO:, :POOL_G])
    n_real = t * tm + 1 + lax.broadcasted_iota(jnp.int32, (tm, 1), 0)
    ys = []
    for g, w in enumerate(POOL_WINDOWS):
        cnt = jnp.minimum(n_real, w).astype(F32)
        diff = sums[g] / cnt - u[:, g * POOL_G:(g + 1) * POOL_G]
        ys.append(_dot(diff.astype(BF16), pw_ref[g]))
    y = jnp.concatenate(ys, axis=-1) * sc_ref[...]
    y_ref[...] = y.astype(BF16)
    carry_ref[...] = u[tm - POOL_HALO:, :]
    st_ref[...] = u[tm - POOL_HALO:, :]


def _pool_prompt(x3, w_in, pool_w, pool_scale, layer, tm):
    bx, lx, _ = x3.shape
    return pl.pallas_call(
        functools.partial(_pool_body, tm=tm),
        grid=(bx, lx // tm),
        in_specs=[pl.BlockSpec((None, tm, D_MODEL), lambda b, t: (b, t, 0)), _layer_spec(w_in, layer),
                  _layer_spec(pool_w, layer), _layer_spec(pool_scale, layer)],
        out_specs=[pl.BlockSpec((None, tm, POOL_W), lambda b, t: (b, t, 0)),
                   pl.BlockSpec((None, POOL_HALO, POOL_W), lambda b, t: (b, 0, 0))],
        out_shape=[jax.ShapeDtypeStruct((bx, lx, POOL_W), BF16),
                   jax.ShapeDtypeStruct((bx, POOL_HALO, POOL_W), F32)],
        scratch_shapes=[pltpu.VMEM((POOL_HALO, POOL_W), F32)],
        compiler_params=_params(("parallel", "arbitrary")),
        name="pool_prompt",
    )(x3, w_in, pool_w, pool_scale)


def _ssm_prep_body(lre_ref, lim_ref, ls_ref, bre_ref, bim_ref, are_ref, aim_ref, bbr_ref, bbi_ref):
    lr, li = lre_ref[...], lim_ref[...]
    step = jnp.exp(ls_ref[...])
    mag = jnp.exp(lr * step)
    ar = mag * jnp.cos(li * step)
    ai = mag * jnp.sin(li * step)
    den = lr * lr + li * li
    fr = ((ar - 1.0) * lr + ai * li) / den
    fi = (ai * lr - (ar - 1.0) * li) / den
    br, bi = bre_ref[...], bim_ref[...]
    are_ref[...] = ar
    aim_ref[...] = ai
    bbr_ref[...] = fr * br - fi * bi
    bbi_ref[...] = fr * bi + fi * br


def _ssm_prep(lam_re, lam_im, log_step, b_re, b_im):
    row = pl.BlockSpec((None, 1, SSM_N), lambda l: (l, 0, 0))
    mat = pl.BlockSpec((None, SSM_CH, SSM_N), lambda l: (l, 0, 0))
    row_sds = jax.ShapeDtypeStruct(lam_re.shape, F32)
    mat_sds = jax.ShapeDtypeStruct(b_re.shape, F32)
    return pl.pallas_call(
        _ssm_prep_body,
        grid=(lam_re.shape[0],),
        in_specs=[row, row, row, mat, mat],
        out_specs=[row, row, mat, mat],
        out_shape=[row_sds, row_sds, mat_sds, mat_sds],
        name="ssm_prep",
    )(lam_re, lam_im, log_step, b_re, b_im)


def _channel_slab(j):
    return j * MXU_TILE // SSM_STATE * SSM_CH // LANES


def _ssm_input(u_slab, j, bre_ref, bim_ref):
    uk = u_slab(_channel_slab(j)).astype(BF16)
    return _dot(uk, bre_ref[j]), _dot(uk, bim_ref[j])


def _ssm_output(h_re, h_im, u_slab, cre_ref, cim_ref, d_ref, wg_ref):
    n_slab = SSM_W // LANES
    acc = [None] * n_slab
    for j in range(SSM_N // MXU_TILE):
        i = _channel_slab(j)
        c = _dot(h_re(j), cre_ref[j]) - _dot(h_im(j), cim_ref[j])
        acc[i] = c if acc[i] is None else acc[i] + c
    y = jnp.concatenate([acc[i] + d_ref[:, i * LANES:(i + 1) * LANES] * u_slab(i) for i in range(n_slab)], axis=-1)
    y = jax.nn.gelu(y)
    return y * jax.nn.sigmoid(_dot(y.astype(BF16), wg_ref[...]))


def _ssm_body(x_ref, win_ref, bre_ref, bim_ref, are_ref, aim_ref, cre_ref, cim_ref, d_ref, wg_ref,
              y_ref, hre_ref, him_ref, u4_ref, sre_ref, sim_ref, y4_ref, *, steps, chunk):
    nbat = x_ref.shape[0]
    n_slab = SSM_W // LANES

    @pl.when(pl.program_id(0) == 0)
    def _():
        hre_ref[...] = jnp.zeros_like(hre_ref)
        him_ref[...] = jnp.zeros_like(him_ref)

    u = _dot(x_ref[...].reshape(nbat * steps, D_MODEL).astype(BF16), win_ref[:, C_SSM:C_GATE])
    for b in range(nbat):
        for s in range(n_slab):
            u4_ref[s, pl.ds(b, steps, stride=nbat), :] = u[b * steps:(b + 1) * steps, s * LANES:(s + 1) * LANES]

    u_slab = lambda s: u4_ref[s]
    for j in range(SSM_N // MXU_TILE):
        cols = slice(j * MXU_TILE, (j + 1) * MXU_TILE)
        sre_ref[:, cols], sim_ref[:, cols] = _ssm_input(u_slab, j, bre_ref, bim_ref)

    for c in range(SSM_N // chunk):
        cols = slice(c * chunk, (c + 1) * chunk)
        ar = jnp.broadcast_to(are_ref[:, cols], (nbat, chunk))
        ai = jnp.broadcast_to(aim_ref[:, cols], (nbat, chunk))

        def step(t, carry, cols=cols, ar=ar, ai=ai):
            hr, hi = carry
            rows = pl.ds(pl.multiple_of(t * nbat, nbat), nbat)
            nr = ar * hr - ai * hi + sre_ref[rows, cols]
            ni = ar * hi + ai * hr + sim_ref[rows, cols]
            sre_ref[rows, cols] = nr
            sim_ref[rows, cols] = ni
            return nr, ni

        hr, hi = lax.fori_loop(0, steps, step, (hre_ref[:, cols], him_ref[:, cols]), unroll=4)
        hre_ref[:, cols] = hr
        him_ref[:, cols] = hi

    def tile(ref):
        return lambda j: ref[:, j * MXU_TILE:(j + 1) * MXU_TILE].astype(BF16)

    y = _ssm_output(tile(sre_ref), tile(sim_ref), u_slab, cre_ref, cim_ref, d_ref, wg_ref)
    for s in range(n_slab):
        y4_ref[s] = y[:, s * LANES:(s + 1) * LANES]
    for b in range(nbat):
        for s in range(n_slab):
            y_ref[b, :, s * LANES:(s + 1) * LANES] = y4_ref[s, pl.ds(b, steps, stride=nbat), :].astype(BF16)


def _ssm_prompt(x3, w_in, sw, layer, steps):
    bx, lx, _ = x3.shape
    assert bx == SUBLANES
    rows = steps * bx
    weights = [w_in, sw["b_re"], sw["b_im"], sw["a_re"], sw["a_im"], sw["c_re"], sw["c_im"], sw["d"], sw["w_glu"]]
    state_spec = pl.BlockSpec((bx, SSM_N), lambda i: (0, 0))
    state_sds = jax.ShapeDtypeStruct((bx, SSM_N), F32)
    return pl.pallas_call(
        functools.partial(_ssm_body, steps=steps, chunk=512),
        grid=(lx // steps,),
        in_specs=[pl.BlockSpec((bx, steps, D_MODEL), lambda i: (0, i, 0))] + [_layer_spec(w, layer) for w in weights],
        out_specs=[pl.BlockSpec((bx, steps, SSM_W), lambda i: (0, i, 0)), state_spec, state_spec],
        out_shape=[jax.ShapeDtypeStruct((bx, lx, SSM_W), BF16), state_sds, state_sds],
        scratch_shapes=[pltpu.VMEM((SSM_W // LANES, rows, LANES), F32), pltpu.VMEM((rows, SSM_N), F32),
                        pltpu.VMEM((rows, SSM_N), F32), pltpu.VMEM((SSM_W // LANES, rows, LANES), F32)],
        compiler_params=_params(("arbitrary",)),
        name="ssm_prompt",
    )(x3, *weights)


def _decode_attn_body(q_ref, k_ref, v_ref, k0_ref, v0_ref, k1_ref, v1_ref, k2_ref, v2_ref, o_ref, lse_ref, *, bt):
    caches = ((k0_ref, v0_ref), (k1_ref, v1_ref), (k2_ref, v2_ref))
    hrow = lax.broadcasted_iota(jnp.int32, (SUBLANES, GROUP_W), 0)
    own_head = lax.broadcasted_iota(jnp.int32, (SUBLANES, GROUP_W), 1) // HEAD_DIM == hrow
    for g, (_, dil) in enumerate(DIL_GROUPS):
        kt_ref, vt_ref = caches[g]
        lb = kt_ref.shape[-1]
        cols = slice(g * GROUP_W, (g + 1) * GROUP_W)
        back = lb - lax.broadcasted_iota(jnp.int32, (SUBLANES, lb), 1)
        srow = lax.broadcasted_iota(jnp.int32, (SUBLANES, lb), 0)
        slope = jnp.zeros((SUBLANES, lb), F32)
        for h in range(HEADS):
            slope = jnp.where(srow == h, float(_ALIBI[g, h]), slope)
        valid = ((back & (dil - 1)) == 0) & (back <= N_BACK * dil)
        bias = jnp.where(valid, -slope * back.astype(F32), -jnp.inf)
        for b in range(bt):
            q, kn, vn = q_ref[b, :, cols], k_ref[b, :, cols], v_ref[b, :, cols]
            qm = jnp.where(own_head, jnp.broadcast_to(q, (SUBLANES, GROUP_W)), 0.0)
            s = _dot(qm.astype(BF16), kt_ref[b].astype(BF16)) * SM_SCALE + bias
            s_new = jnp.sum(qm * kn, axis=-1, keepdims=True) * SM_SCALE
            m = jnp.maximum(jnp.max(s, axis=-1, keepdims=True), s_new)
            p = jnp.exp(s - m)
            p_new = jnp.exp(s_new - m)
            l = jnp.sum(p, axis=-1, keepdims=True) + p_new
            o = lax.dot_general(p.astype(BF16), vt_ref[b].astype(BF16), (((1,), (1,)), ((), ())),
                                preferred_element_type=F32) + p_new * vn
            o_ref[b, :, cols] = jnp.sum(jnp.where(own_head, o / l, 0.0), axis=0, keepdims=True)
            lse_ref[b, :, cols] = jnp.sum(jnp.where(own_head, m + jnp.log(l), 0.0), axis=0, keepdims=True)


def _decode_attn(layer, q, k, v, caches, bt):
    nb = q.shape[0]
    row1 = pl.BlockSpec((bt, 1, ATT_W), lambda i: (i, 0, 0))
    views, specs = [], []
    for g, (window, dil) in enumerate(DIL_GROUPS):
        assert dil & (dil - 1) == 0
        for a in caches[2 * g:2 * g + 2]:
            lb = a.shape[2]
            assert lb == N_BACK * dil
            views.append(a.transpose(0, 1, 3, 4, 2).reshape(DEPTH, nb, GROUP_W, lb))
            specs.append(pl.BlockSpec((None, bt, GROUP_W, lb), lambda i: (layer, i, 0, 0)))
    sds = jax.ShapeDtypeStruct((nb, 1, ATT_W), F32)
    return pl.pallas_call(
        functools.partial(_decode_attn_body, bt=bt),
        grid=(nb // bt,),
        in_specs=[row1, row1, row1] + specs,
        out_specs=[row1, row1],
        out_shape=[sds, sds],
        compiler_params=_params(("parallel",)),
        name="decode_attn",
    )(q, k, v, *views)


def _decode_mix_body(up_ref, us_ref, buf_ref, h0r_ref, h0i_ref,
                     pw_ref, sc_ref, bre_ref, bim_ref, are_ref, aim_ref, cre_ref, cim_ref, d_ref, wg_ref,
                     yp_ref, ys_ref, hr_ref, hi_ref):
    up = up_ref[...]
    ys = []
    for g, w in enumerate(POOL_WINDOWS):
        gcols = slice(g * POOL_G, (g + 1) * POOL_G)
        tot = up[:, gcols]
        for i in range(POOL_BUF - (w - 1), POOL_BUF):
            tot = tot + buf_ref[i, :, gcols]
        diff = tot / float(min(PAST_LEN + 1, w)) - up[:, gcols]
        ys.append(_dot(diff.astype(BF16), pw_ref[g]))
    yp_ref[...] = (jnp.concatenate(ys, axis=-1) * sc_ref[...]).astype(BF16)

    u_slab = lambda s: us_ref[:, s * LANES:(s + 1) * LANES]
    h_re, h_im = [], []
    for j in range(SSM_N // MXU_TILE):
        cols = slice(j * MXU_TILE, (j + 1) * MXU_TILE)
        in_re, in_im = _ssm_input(u_slab, j, bre_ref, bim_ref)
        ar, ai = are_ref[:, cols], aim_ref[:, cols]
        pr, pi = h0r_ref[:, cols], h0i_ref[:, cols]
        nr = ar * pr - ai * pi + in_re
        ni = ar * pi + ai * pr + in_im
        hr_ref[:, cols] = nr
        hi_ref[:, cols] = ni
        h_re.append(nr.astype(BF16))
        h_im.append(ni.astype(BF16))
    y = _ssm_output(h_re.__getitem__, h_im.__getitem__, u_slab, cre_ref, cim_ref, d_ref, wg_ref)
    ys_ref[...] = y.astype(BF16)


def _decode_mix(layer, up, us, state_pool, h0_re, h0_im, pool_w, pool_scale, sw, bt):
    nb = up.shape[0]
    row = lambda c: pl.BlockSpec((bt, c), lambda i: (i, 0))
    buf = state_pool.transpose(0, 2, 1, 3)
    h0r = h0_re.reshape(DEPTH, nb, SSM_N)
    h0i = h0_im.reshape(DEPTH, nb, SSM_N)
    lrow = lambda c: pl.BlockSpec((None, bt, c), lambda i: (layer, i, 0))
    weights = [pool_w, pool_scale, sw["b_re"], sw["b_im"], sw["a_re"], sw["a_im"], sw["c_re"], sw["c_im"],
               sw["d"], sw["w_glu"]]
    return pl.pallas_call(
        _decode_mix_body,
        grid=(nb // bt,),
        in_specs=[row(POOL_W), row(SSM_W),
                  pl.BlockSpec((None, POOL_BUF, bt, POOL_W), lambda i: (layer, 0, i, 0)),
                  lrow(SSM_N), lrow(SSM_N)] + [_layer_spec(w, layer) for w in weights],
        out_specs=[row(POOL_W), row(SSM_W), row(SSM_N), row(SSM_N)],
        out_shape=[jax.ShapeDtypeStruct((nb, POOL_W), BF16), jax.ShapeDtypeStruct((nb, SSM_W), BF16),
                   jax.ShapeDtypeStruct((nb, SSM_N), F32), jax.ShapeDtypeStruct((nb, SSM_N), F32)],
        compiler_params=_params(("parallel",)),
        name="decode_mix",
    )(up, us, buf, h0r, h0i, *weights)


def _merge_body(x_ref, o0_ref, o1_ref, o2_ref, l0_ref, l1_ref, l2_ref, yp_ref, ys_ref,
                win_ref, wa_ref, wp_ref, ws_ref, wo_ref, g_ref, b_ref, out_ref):
    x = x_ref[...]
    xb = x.astype(BF16)
    l0, l1, l2 = l0_ref[...], l1_ref[...], l2_ref[...]
    m = jnp.maximum(jnp.maximum(l0, l1), l2)
    e0, e1, e2 = jnp.exp(l0 - m), jnp.exp(l1 - m), jnp.exp(l2 - m)
    att = (e0 * o0_ref[...] + e1 * o1_ref[...] + e2 * o2_ref[...]) / (e0 + e1 + e2)

    def gate(i):
        return jax.nn.sigmoid(_dot(xb, win_ref[:, C_GATE + i * D_MODEL:C_GATE + (i + 1) * D_MODEL]))

    merged = gate(0) * _dot(att.astype(BF16), wa_ref[...])
    merged = merged + gate(1) * _dot(yp_ref[...], wp_ref[...])
    merged = merged + gate(2) * _dot(ys_ref[...], ws_ref[...])
    mix = _dot(merged.astype(BF16), wo_ref[...])
    out_ref[...] = _layer_norm(ALPHA * x + mix, g_ref[...], b_ref[...])


def _merge(x, o_specs, o_arrays, l_arrays, yp, ys, lw, layer, tm):
    m = x.shape[0]
    weights = [lw["w_in"], lw["w_br_att"], lw["w_br_pool"], lw["w_br_ssm"], lw["w_out"], lw["ln1_g"], lw["ln1_b"]]
    return pl.pallas_call(
        _merge_body,
        grid=(m // tm,),
        in_specs=[_rows_spec(tm, D_MODEL)] + o_specs + o_specs
                 + [_rows_spec(tm, POOL_W), _rows_spec(tm, SSM_W)] + [_layer_spec(w, layer) for w in weights],
        out_specs=_rows_spec(tm, D_MODEL),
        out_shape=jax.ShapeDtypeStruct(x.shape, F32),
        compiler_params=_params(("parallel",)),
        name="merge",
    )(x, *o_arrays, *l_arrays, yp, ys, *weights)


def _ffn_body(x_ref, wg_ref, wu_ref, wd_ref, g_ref, b_ref, out_ref):
    x = x_ref[...]
    xb = x.astype(BF16)
    f = None
    for c in range(D_FF // FF_CHUNK):
        cols = slice(c * FF_CHUNK, (c + 1) * FF_CHUNK)
        h = jax.nn.silu(_dot(xb, wg_ref[:, cols])) * _dot(xb, wu_ref[:, cols])
        part = _dot(h.astype(BF16), wd_ref[cols, :])
        f = part if f is None else f + part
    out_ref[...] = _layer_norm(ALPHA * x + f, g_ref[...], b_ref[...])


def _ffn(x, lw, layer, tm):
    m = x.shape[0]
    weights = [lw["ffn_w_gate"], lw["ffn_w_up"], lw["ffn_w_down"], lw["ln2_g"], lw["ln2_b"]]
    return pl.pallas_call(
        _ffn_body,
        grid=(m // tm,),
        in_specs=[_rows_spec(tm, D_MODEL)] + [_layer_spec(w, layer) for w in weights],
        out_specs=_rows_spec(tm, D_MODEL),
        out_shape=jax.ShapeDtypeStruct(x.shape, F32),
        compiler_params=_params(("parallel",)),
        name="ffn",
    )(x, *weights)


def _diag_tiles(blocks, per):
    dp, g, r, c = blocks.shape
    t = blocks.reshape(dp, g // per, per, r, c)
    eye = jnp.eye(per, dtype=bool)[None, None, :, None, :, None]
    return jnp.where(eye, t[:, :, :, :, None, :], 0).reshape(dp, g // per, per * r, per * c)


def _ssm_weights(a_re, a_im, bb_re, bb_im, ssm_c_re, ssm_c_im, ssm_d, ssm_w_glu):
    per = MXU_TILE // SSM_STATE
    n_tiles = SSM_GROUPS // per
    half = (jnp.arange(n_tiles)[:, None] * (per * SSM_CH) // (LANES // 2)) % 2 == jnp.arange(2)[None, :]
    from_lanes = lambda b: b.reshape(DEPTH, SSM_CH, SSM_GROUPS, SSM_STATE).transpose(0, 2, 1, 3)

    def b_tiles(bb):
        t = _diag_tiles(from_lanes(bb), per)
        t = jnp.where(half[None, :, :, None, None], t[:, :, None], 0)
        return t.reshape(DEPTH, n_tiles, LANES, MXU_TILE).astype(BF16)

    def c_tiles(c):
        t = _diag_tiles(c.transpose(0, 1, 3, 2), per)
        t = jnp.where(half[None, :, None, :, None], t[:, :, :, None, :], 0)
        return t.reshape(DEPTH, n_tiles, MXU_TILE, LANES).astype(BF16)

    return dict(b_re=b_tiles(bb_re), b_im=b_tiles(bb_im), c_re=c_tiles(ssm_c_re), c_im=c_tiles(ssm_c_im),
                a_re=a_re, a_im=a_im, d=ssm_d.reshape(DEPTH, 1, -1), w_glu=ssm_w_glu.astype(BF16))


def kernel(x_prompt, x_sample, cache_k_w128, cache_v_w128, cache_k_w512, cache_v_w512, cache_k_w2048,
           cache_v_w2048, state_pool, state_ssm_re, state_ssm_im, w_in, pool_w, pool_scale, ssm_lambda_re,
           ssm_lambda_im, ssm_b_re, ssm_b_im, ssm_c_re, ssm_c_im, ssm_d, ssm_log_step, ssm_w_glu, w_br_att,
           w_br_pool, w_br_ssm, w_out, ln1_g, ln1_b, ffn_w_gate, ffn_w_up, ffn_w_down, ln2_g, ln2_b):
    bp, lp, _ = x_prompt.shape
    nb = x_sample.shape[0]
    caches = [cache_k_w128, cache_v_w128, cache_k_w512, cache_v_w512, cache_k_w2048, cache_v_w2048]

    flat = lambda a: a.reshape(DEPTH, 1, SSM_N)
    to_lanes = lambda b: b.transpose(0, 3, 1, 2).reshape(DEPTH, SSM_CH, SSM_N)
    log_step = jnp.repeat(ssm_log_step, SSM_STATE, axis=-1)
    a_re, a_im, bb_re, bb_im = _ssm_prep(flat(ssm_lambda_re), flat(ssm_lambda_im), flat(log_step),
                                         to_lanes(ssm_b_re), to_lanes(ssm_b_im))
    sw = _ssm_weights(a_re, a_im, bb_re, bb_im, ssm_c_re, ssm_c_im, ssm_d, ssm_w_glu)
    row = lambda a: a.reshape(DEPTH, 1, -1)
    w_in_b = w_in.astype(BF16)
    lw = dict(w_in=w_in_b, w_br_att=w_br_att.astype(BF16), w_br_pool=w_br_pool.astype(BF16),
              w_br_ssm=w_br_ssm.astype(BF16), w_out=w_out.astype(BF16), ln1_g=row(ln1_g), ln1_b=row(ln1_b),
              ffn_w_gate=ffn_w_gate.astype(BF16), ffn_w_up=ffn_w_up.astype(BF16),
              ffn_w_down=ffn_w_down.astype(BF16), ln2_g=row(ln2_g), ln2_b=row(ln2_b))
    pw = pool_w.astype(BF16)
    psc = row(pool_scale)

    xp = x_prompt.reshape(bp * lp, D_MODEL)
    xs = x_sample.reshape(nb, D_MODEL)
    st_p = [[] for _ in range(9)]
    st_s = [[] for _ in range(9)]
    for l in range(DEPTH):
        x3 = xp.reshape(bp, lp, D_MODEL)
        as3 = lambda a: a.reshape(bp, lp, a.shape[-1])
        q_p, k_p, v_p, kv_t = _project_qkv(xp, w_in_b, l, bp, 512)
        att = [_attn_prompt(as3(q_p), as3(k_p[g]), as3(v_p[g]), g, nq) for g, nq in enumerate((4, 1, 1))]
        y_pool, pool_rows = _pool_prompt(x3, w_in_b, pw, psc, l, 1024)
        y_ssm, h_re, h_im = _ssm_prompt(x3, w_in_b, sw, l, 64)
        flat2 = lambda a: a.reshape(bp * lp, a.shape[-1])
        x1 = _merge(xp, [_rows_spec(512, GROUP_W)] * N_GROUPS, [flat2(a[0]) for a in att],
                    [flat2(a[1]) for a in att], flat2(y_pool), flat2(y_ssm), lw, l, 512)
        xp = _ffn(x1, lw, l, 512)
        for g, (window, _) in enumerate(DIL_GROUPS):
            keep = min(window, lp)
            for j, a in enumerate((k_p[g], v_p[g])):
                if g == N_GROUPS - 1 and keep == lp:
                    st_p[2 * g + j].append(kv_t[j].reshape(bp, HEADS, HEAD_DIM, lp).transpose(0, 3, 1, 2))
                else:
                    st_p[2 * g + j].append(as3(a)[:, lp - keep:].reshape(bp, keep, HEADS, HEAD_DIM))
        st_p[6].append(pool_rows[:, POOL_HALO - POOL_BUF:])
        st_p[7].append(h_re.reshape(bp, SSM_GROUPS, SSM_STATE))
        st_p[8].append(h_im.reshape(bp, SSM_GROUPS, SSM_STATE))

        proj = _project(xs, w_in_b, l, C_GATE, nb)
        q_s, k_s, v_s = proj[:, :ATT_W], proj[:, ATT_W:2 * ATT_W], proj[:, 2 * ATT_W:QKV_W]
        up_s, us_s = proj[:, C_POOL:C_SSM], proj[:, C_SSM:C_GATE]
        as_rows = lambda a: a.reshape(nb, 1, ATT_W)
        o_s, lse_s = _decode_attn(l, as_rows(q_s), as_rows(k_s), as_rows(v_s), caches, 2)
        yp_s, ys_s, hr_s, hi_s = _decode_mix(l, up_s, us_s, state_pool, state_ssm_re, state_ssm_im,
                                             pw, psc, sw, nb)
        o_specs = [_rows_spec(nb, GROUP_W, col=g) for g in range(N_GROUPS)]
        x1s = _merge(xs, o_specs, [o_s.reshape(nb, ATT_W)] * N_GROUPS, [lse_s.reshape(nb, ATT_W)] * N_GROUPS,
                     yp_s, ys_s, lw, l, nb)
        xs = _ffn(x1s, lw, l, nb)
        for g in range(N_GROUPS):
            cols = slice(g * GROUP_W, (g + 1) * GROUP_W)
            st_s[2 * g].append(k_s[:, cols].reshape(nb, 1, HEADS, HEAD_DIM))
            st_s[2 * g + 1].append(v_s[:, cols].reshape(nb, 1, HEADS, HEAD_DIM))
        st_s[6].append(up_s.reshape(nb, 1, POOL_W))
        st_s[7].append(hr_s.reshape(nb, SSM_GROUPS, SSM_STATE))
        st_s[8].append(hi_s.reshape(nb, SSM_GROUPS, SSM_STATE))

    return (xp.reshape(bp, lp, D_MODEL), xs.reshape(nb, 1, D_MODEL),
            *[jnp.stack(s, axis=0) for s in st_p], *[jnp.stack(s, axis=0) for s in st_s])
```
